```python
import jax, jax.numpy as jnp
from jax import lax
import numpy as np

D_MODEL = 1024
BATCH = 32
SEQ = 2048
DEPTH = 1
DEC_BATCH = 32
DEC_SEQ = 64
PAST_LEN = 1024

CHUNK = 64
POOL_WINDOWS = (2, 4, 8, 16)
N_POOL_GROUPS = len(POOL_WINDOWS)
D_POOL = D_MODEL // 2
POOL_GROUP = D_POOL // N_POOL_GROUPS
POOL_HIST = max(POOL_WINDOWS) - 1
D_CONV = D_MODEL // 2
CONV_WIDTH = 31
CONV_HIST = CONV_WIDTH - 1
N_BRANCH = 2
D_IN = D_POOL + 2 * D_CONV + N_BRANCH * D_MODEL
D_FF = 4 * D_MODEL
EPS = 1e-6

kernel_name = 'pool_conformer_hybrid_stream_step'


def _rmsnorm(x, g):
    xf = x.astype(jnp.float32)
    y = xf * lax.rsqrt(jnp.mean(xf * xf, axis=-1, keepdims=True) + EPS)
    return (y * g.astype(jnp.float32)).astype(x.dtype)


def _layernorm(x, g, b):
    xf = x.astype(jnp.float32)
    mu = jnp.mean(xf, axis=-1, keepdims=True)
    xc = xf - mu
    var = jnp.mean(xc * xc, axis=-1, keepdims=True)
    y = xc * lax.rsqrt(var + EPS) * g.astype(jnp.float32) + b.astype(jnp.float32)
    return y.astype(x.dtype)


def _adaln(c, w, b):
    mod = jax.nn.silu(c) @ w + b
    shift, scale, gate = jnp.split(mod, 3, axis=-1)
    return shift[:, None, :], scale[:, None, :], gate[:, None, :]


def _pool_mixer(u_ext, pos0, w_grp, pool_scale):
    bsz = u_ext.shape[0]
    t_len = u_ext.shape[1] - POOL_HIST
    uf = u_ext.astype(jnp.float32)
    cs = jnp.concatenate([jnp.zeros_like(uf[:, :1]), jnp.cumsum(uf, axis=1)], axis=1)
    end = cs[:, POOL_HIST + 1:]
    pos = pos0 + jnp.arange(t_len)
    outs = []
    for g, k in enumerate(POOL_WINDOWS):
        lo, hi = g * POOL_GROUP, (g + 1) * POOL_GROUP
        start = cs[:, POOL_HIST + 1 - k: POOL_HIST + 1 - k + t_len, lo:hi]
        cnt = jnp.minimum(k, pos + 1).astype(jnp.float32)[None, :, None]
        outs.append((end[:, :, lo:hi] - start) / cnt)
    pooled = jnp.concatenate(outs, axis=-1)
    d = (pooled - uf[:, POOL_HIST:]).astype(u_ext.dtype)
    d = d.reshape(bsz, t_len, N_POOL_GROUPS, POOL_GROUP)
    z = jnp.einsum('btgc,gcd->btgd', d, w_grp).reshape(bsz, t_len, D_POOL)
    return z * pool_scale


def _dwconv(v_ext, w_dw, b_dw):
    out = lax.conv_general_dilated(v_ext, w_dw[:, None, :], window_strides=(1,), padding='VALID',
                                   dimension_numbers=('NWC', 'WIO', 'NWC'),
                                   feature_group_count=D_CONV)
    return out + b_dw


def _layer(x, c, pool_hist, conv_hist, pos0,
           w_ada_mix, b_ada_mix, g_pre_mix, g_post_mix, w_in, w_grp, pool_scale, w_pool_proj,
           w_dw, b_dw, ln_g, ln_b, w_conv_proj, w_out,
           w_ada_ffn, b_ada_ffn, g_pre_ffn, g_post_ffn, w_ff1, w_ff2):
    shift, scale, gate = _adaln(c, w_ada_mix, b_ada_mix)
    h = _rmsnorm(x, g_pre_mix) * (1 + scale) + shift
    p = h @ w_in
    u_a, v_in, gate_logits = jnp.split(p, [D_POOL, D_POOL + 2 * D_CONV], axis=-1)
    u_ext = jnp.concatenate([pool_hist, u_a], axis=1)
    y_a = _pool_mixer(u_ext, pos0, w_grp, pool_scale) @ w_pool_proj
    v = v_in[..., :D_CONV] * jax.nn.sigmoid(v_in[..., D_CONV:])
    v_ext = jnp.concatenate([conv_hist, v], axis=1)
    z = jax.nn.silu(_layernorm(_dwconv(v_ext, w_dw, b_dw), ln_g, ln_b))
    y_b = z @ w_conv_proj
    g_a, g_b = jnp.split(jax.nn.sigmoid(gate_logits), 2, axis=-1)
    out = (g_a * y_a + g_b * y_b) @ w_out
    x = x + gate * _rmsnorm(out, g_post_mix)
    shift, scale, gate = _adaln(c, w_ada_ffn, b_ada_ffn)
    h = _rmsnorm(x, g_pre_ffn) * (1 + scale) + shift
    f = jnp.square(jax.nn.relu(h @ w_ff1)) @ w_ff2
    x = x + gate * _rmsnorm(f, g_post_ffn)
    return x, u_ext[:, -POOL_HIST:], v_ext[:, -CONV_HIST:]


def setup_inputs(seed: int = 0) -> dict:
    key = jax.random.key(seed)
    ks = jax.random.split(key, 32)
    nrm = lambda k, shape, s: jax.random.normal(k, shape, jnp.float32) * s
    L, D = DEPTH, D_MODEL
    return {
        'x_prompt': nrm(ks[0], (BATCH, SEQ, D), 1.0),
        'x_sample': nrm(ks[1], (DEC_BATCH, DEC_SEQ, D), 1.0),
        'state_pool': nrm(ks[2], (L, DEC_BATCH, POOL_HIST, D_POOL), 1.0),
        'state_conv': nrm(ks[3], (L, DEC_BATCH, CONV_HIST, D_CONV), 0.5),
        'c_prompt': nrm(ks[4], (BATCH, D), 1.0),
        'c_sample': nrm(ks[5], (DEC_BATCH, D), 1.0),
        'w_ada_mix': nrm(ks[6], (L, D, 3 * D), 0.5 * D ** -0.5),
        'b_ada_mix': nrm(ks[7], (L, 3 * D), 0.01),
        'g_pre_mix': 1.0 + nrm(ks[8], (L, D), 0.05),
        'g_post_mix': 1.0 + nrm(ks[9], (L, D), 0.05),
        'w_in': nrm(ks[10], (L, D, D_IN), D ** -0.5),
        'w_grp': nrm(ks[11], (L, N_POOL_GROUPS, POOL_GROUP, POOL_GROUP), POOL_GROUP ** -0.5),
        'pool_scale': 1.0 + nrm(ks[12], (L, D_POOL), 0.1),
        'w_pool_proj': nrm(ks[13], (L, D_POOL, D), D_POOL ** -0.5),
        'w_dw': nrm(ks[14], (L, CONV_WIDTH, D_CONV), CONV_WIDTH ** -0.5),
        'b_dw': nrm(ks[15], (L, D_CONV), 0.01),
        'ln_g': 1.0 + nrm(ks[16], (L, D_CONV), 0.05),
        'ln_b': nrm(ks[17], (L, D_CONV), 0.01),
        'w_conv_proj': nrm(ks[18], (L, D_CONV, D), D_CONV ** -0.5),
        'w_out': nrm(ks[19], (L, D, D), D ** -0.5),
        'w_ada_ffn': nrm(ks[20], (L, D, 3 * D), 0.5 * D ** -0.5),
        'b_ada_ffn': nrm(ks[21], (L, 3 * D), 0.01),
        'g_pre_ffn': 1.0 + nrm(ks[22], (L, D), 0.05),
        'g_post_ffn': 1.0 + nrm(ks[23], (L, D), 0.05),
        'w_ff1': nrm(ks[24], (L, D, D_FF), D ** -0.5),
        'w_ff2': nrm(ks[25], (L, D_FF, D), D_FF ** -0.5),
    }


def reference(x_prompt, x_sample, state_pool, state_conv, c_prompt, c_sample,
              w_ada_mix, b_ada_mix, g_pre_mix, g_post_mix, w_in, w_grp, pool_scale, w_pool_proj,
              w_dw, b_dw, ln_g, ln_b, w_conv_proj, w_out,
              w_ada_ffn, b_ada_ffn, g_pre_ffn, g_post_ffn, w_ff1, w_ff2):
    xp, xs = x_prompt, x_sample
    bp = x_prompt.shape[0]
    pool_p, conv_p, pool_s, conv_s = [], [], [], []
    for l in range(DEPTH):
        lw = (w_ada_mix[l], b_ada_mix[l], g_pre_mix[l], g_post_mix[l], w_in[l], w_grp[l],
              pool_scale[l], w_pool_proj[l], w_dw[l], b_dw[l], ln_g[l], ln_b[l],
              w_conv_proj[l], w_out[l], w_ada_ffn[l], b_ada_ffn[l], g_pre_ffn[l],
              g_post_ffn[l], w_ff1[l], w_ff2[l])
        zero_pool = jnp.zeros((bp, POOL_HIST, D_POOL), xp.dtype)
        zero_conv = jnp.zeros((bp, CONV_HIST, D_CONV), xp.dtype)
        xp, sp, cp = _layer(xp, c_prompt, zero_pool, zero_conv, 0, *lw)
        xs, ss, cs = _layer(xs, c_sample, state_pool[l], state_conv[l], PAST_LEN, *lw)
        pool_p.append(sp); conv_p.append(cp); pool_s.append(ss); conv_s.append(cs)
    new_pool_prompt = jnp.stack(pool_p, axis=0)
    new_conv_prompt = jnp.stack(conv_p, axis=0)
    new_pool_sample = jnp.stack(pool_s, axis=0)
    new_conv_sample = jnp.stack(conv_s, axis=0)
    return (xp, xp_sample_out(xs) if False else xs, new_pool_prompt, new_conv_prompt, new_pool_sample, new_conv_sample)
```

```python
import functools

import jax
import jax.numpy as jnp
from jax import lax
from jax.experimental import pallas as pl
from jax.experimental.pallas import tpu as pltpu

D_MODEL = 1024
POOL_WINDOWS = (2, 4, 8, 16)
POOL_GROUP = 128
D_POOL = POOL_GROUP * len(POOL_WINDOWS)
POOL_HIST = max(POOL_WINDOWS) - 1
D_CONV = 512
CONV_WIDTH = 31
CONV_HIST = CONV_WIDTH - 1
D_IN = D_POOL + 2 * D_CONV + 2 * D_MODEL
D_FF = 4 * D_MODEL
EPS = 1e-6
PAST_LEN = 1024

POOL_PAD = 16
CONV_PAD = 32
CONV_ROWS = 32
FF_CHUNK = 1024
ADA_TILE = 512
VMEM_LIMIT_BYTES = 56 * 1024 * 1024

_F32 = jnp.float32
_BF16 = jnp.bfloat16


def _dot(a, b):
    return jnp.dot(a, b, preferred_element_type=_F32)


def _sigmoid(x):
    return 1.0 / (1.0 + jnp.exp(-x))


def _rms(x, g):
    return x * lax.rsqrt(jnp.mean(x * x, axis=-1, keepdims=True) + EPS) * g


def _adaln_body(c_ref, wm_ref, bm_ref, wf_ref, bf_ref, om_ref, of_ref):
    c = c_ref[...]
    s = (c * _sigmoid(c)).astype(_BF16)
    om_ref[...] = _dot(s, wm_ref[...].astype(_BF16)) + bm_ref[...]
    of_ref[...] = _dot(s, wf_ref[...].astype(_BF16)) + bf_ref[...]


def _adaln(c, w_mix, b_mix, w_ffn, b_ffn):
    n = c.shape[0]
    n_out = w_mix.shape[1]
    w_spec = pl.BlockSpec((D_MODEL, ADA_TILE), lambda j: (0, j))
    b_spec = pl.BlockSpec((1, ADA_TILE), lambda j: (0, j))
    o_spec = pl.BlockSpec((n, ADA_TILE), lambda j: (0, j))
    return pl.pallas_call(
        _adaln_body,
        grid=(n_out // ADA_TILE,),
        in_specs=[pl.BlockSpec((n, D_MODEL), lambda j: (0, 0)), w_spec, b_spec, w_spec, b_spec],
        out_specs=[o_spec, o_spec],
        out_shape=[jax.ShapeDtypeStruct((n, n_out), _F32)] * 2,
        name="adaln",
    )(c, w_mix, b_mix, w_ffn, b_ffn)


def _layer_body(*refs, bt, tt, pos0, has_state):
    refs = list(refs)
    x_ref, mm_ref, mf_ref = refs[:3]
    refs = refs[3:]
    if has_state:
        sp_ref, sc_ref = refs[:2]
        refs = refs[2:]
    (g_pre_mix, g_post_mix, w_in, w_grp, pool_scale, w_pool_proj, w_dw, b_dw, ln_g, ln_b,
     w_conv_proj, w_out, g_pre_ffn, g_post_ffn, w_ff1, w_ff2,
     y_ref, npool_ref, nconv_ref, ubuf, vbuf, zbuf) = refs

    m = bt * tt
    t_idx = pl.program_id(1)

    @pl.when(t_idx == 0)
    def _init_history():
        ubuf[:, 0:POOL_PAD, :] = jnp.zeros((bt, POOL_PAD, D_POOL), _F32)
        vbuf[:, 0:CONV_PAD, :] = jnp.zeros((bt, CONV_PAD, D_CONV), _F32)
        if has_state:
            ubuf[:, POOL_PAD - POOL_HIST:POOL_PAD, :] = sp_ref[...]
            vbuf[:, CONV_PAD - CONV_HIST:CONV_PAD, :] = sc_ref[...]

    def split3(mod_ref):
        mod = mod_ref[...]
        return (mod[:, :, 0:D_MODEL], mod[:, :, D_MODEL:2 * D_MODEL],
                mod[:, :, 2 * D_MODEL:3 * D_MODEL])

    x = x_ref[...]

    shift, scale, gate = split3(mm_ref)
    h = (_rms(x, g_pre_mix[...]) * (1.0 + scale) + shift).astype(_BF16).reshape(m, D_MODEL)

    u = _dot(h, w_in[:, 0:D_POOL])
    ubuf[:, POOL_PAD:POOL_PAD + tt, :] = u.reshape(bt, tt, D_POOL)
    pos = pos0 + t_idx * tt + lax.broadcasted_iota(jnp.int32, (tt, POOL_GROUP), 0)
    z_parts = []
    for g, k in enumerate(POOL_WINDOWS):
        lo, hi = g * POOL_GROUP, (g + 1) * POOL_GROUP
        cur = ubuf[:, POOL_PAD:POOL_PAD + tt, lo:hi]
        s = cur
        for j in range(1, k):
            s = s + ubuf[:, POOL_PAD - j:POOL_PAD - j + tt, lo:hi]
        cnt = jnp.minimum(k, pos + 1).astype(_F32)
        d = (s / cnt - cur).astype(_BF16).reshape(m, POOL_GROUP)
        z_parts.append(_dot(d, w_grp[g]) * pool_scale[:, lo:hi])
    z_a = jnp.concatenate(z_parts, axis=-1).astype(_BF16)
    y_a = _dot(z_a, w_pool_proj[...])

    v = _dot(h, w_in[:, D_POOL:D_POOL + D_CONV]) * _sigmoid(
        _dot(h, w_in[:, D_POOL + D_CONV:D_POOL + 2 * D_CONV]))
    vbuf[:, CONV_PAD:CONV_PAD + tt, :] = v.reshape(bt, tt, D_CONV)
    base = CONV_PAD - CONV_HIST
    for b in range(bt):
        for r0 in range(0, tt, CONV_ROWS):
            acc = jnp.broadcast_to(b_dw[...], (CONV_ROWS, D_CONV))
            for j in range(CONV_WIDTH):
                acc = acc + w_dw[j:j + 1, :] * vbuf[b, base + r0 + j:base + r0 + j + CONV_ROWS, :]
            mu = jnp.mean(acc, axis=-1, keepdims=True)
            xc = acc - mu
            var = jnp.mean(xc * xc, axis=-1, keepdims=True)
            ln = xc * lax.rsqrt(var + EPS) * ln_g[...] + ln_b[...]
            zbuf[b * tt + r0:b * tt + r0 + CONV_ROWS, :] = (ln * _sigmoid(ln)).astype(_BF16)
    y_b = _dot(zbuf[...], w_conv_proj[...])

    off = D_POOL + 2 * D_CONV
    g_a = _sigmoid(_dot(h, w_in[:, off:off + D_MODEL]))
    g_b = _sigmoid(_dot(h, w_in[:, off + D_MODEL:off + 2 * D_MODEL]))
    merged = (g_a * y_a + g_b * y_b).astype(_BF16)
    out = _dot(merged, w_out[...]).reshape(bt, tt, D_MODEL)
    x1 = x + gate * _rms(out, g_post_mix[...])

    shift, scale, gate = split3(mf_ref)
    h2 = (_rms(x1, g_pre_ffn[...]) * (1.0 + scale) + shift).astype(_BF16).reshape(m, D_MODEL)
    f = jnp.zeros((m, D_MODEL), _F32)
    for c0 in range(0, D_FF, FF_CHUNK):
        a = jnp.maximum(_dot(h2, w_ff1[:, c0:c0 + FF_CHUNK]), 0.0)
        f = f + _dot((a * a).astype(_BF16), w_ff2[c0:c0 + FF_CHUNK, :])
    y_ref[...] = x1 + gate * _rms(f.reshape(bt, tt, D_MODEL), g_post_ffn[...])

    @pl.when(t_idx == pl.num_programs(1) - 1)
    def _write_state():
        npool_ref[...] = ubuf[:, POOL_PAD + tt - POOL_HIST:POOL_PAD + tt, :]
        nconv_ref[...] = vbuf[:, CONV_PAD + tt - CONV_HIST:CONV_PAD + tt, :]

    ubuf[:, 0:POOL_PAD, :] = ubuf[:, tt:tt + POOL_PAD, :]
    vbuf[:, 0:CONV_PAD, :] = vbuf[:, tt:tt + CONV_PAD, :]


def _tile_rows(batch, seq):
    rows = 256
    tt = min(seq, rows)
    bt = max(1, min(batch, rows // tt))
    return bt, tt


def _layer(x, mod_mix, mod_ffn, state, pos0, params):
    batch, seq, _ = x.shape
    bt, tt = _tile_rows(batch, seq)
    assert batch % bt == 0 and seq % tt == 0 and tt % CONV_ROWS == 0 and tt >= CONV_PAD
    has_state = state is not None
    m = bt * tt

    def tile_spec(rows, cols):
        return pl.BlockSpec((bt, rows, cols), lambda b, t: (b, 0, 0))

    def whole(a):
        nd = a.ndim
        return pl.BlockSpec(a.shape, lambda b, t: (0,) * nd, pipeline_mode=pl.Buffered(1))

    in_specs = [pl.BlockSpec((bt, tt, D_MODEL), lambda b, t: (b, t, 0)),
                tile_spec(1, 3 * D_MODEL), tile_spec(1, 3 * D_MODEL)]
    args = [x, mod_mix, mod_ffn]
    if has_state:
        in_specs += [tile_spec(POOL_HIST, D_POOL), tile_spec(CONV_HIST, D_CONV)]
        args += list(state)
    in_specs += [whole(p) for p in params]
    args += list(params)

    return pl.pallas_call(
        functools.partial(_layer_body, bt=bt, tt=tt, pos0=pos0, has_state=has_state),
        grid=(batch // bt, seq // tt),
        in_specs=in_specs,
        out_specs=[pl.BlockSpec((bt, tt, D_MODEL), lambda b, t: (b, t, 0)),
                   tile_spec(POOL_HIST, D_POOL), tile_spec(CONV_HIST, D_CONV)],
        out_shape=[jax.ShapeDtypeStruct(x.shape, _F32),
                   jax.ShapeDtypeStruct((batch, POOL_HIST, D_POOL), _F32),
                   jax.ShapeDtypeStruct((batch, CONV_HIST, D_CONV), _F32)],
        scratch_shapes=[pltpu.VMEM((bt, POOL_PAD + tt, D_POOL), _F32),
                        pltpu.VMEM((bt, CONV_PAD + tt, D_CONV), _F32),
                        pltpu.VMEM((m, D_CONV), _BF16)],
        compiler_params=pltpu.CompilerParams(
            dimension_semantics=("arbitrary", "arbitrary"),
            vmem_limit_bytes=VMEM_LIMIT_BYTES),
        name="layer_state" if has_state else "layer",
    )(*args)


def kernel(x_prompt, x_sample, state_pool, state_conv, c_prompt, c_sample, w_ada_mix, b_ada_mix, g_pre_mix, g_post_mix, w_in, w_grp, pool_scale, w_pool_proj, w_dw, b_dw, ln_g, ln_b, w_conv_proj, w_out, w_ada_ffn, b_ada_ffn, g_pre_ffn, g_post_ffn, w_ff1, w_ff2):
    depth = w_in.shape[0]
    assert depth == 1
    l = 0
    bp = x_prompt.shape[0]
    bs = x_sample.shape[0]

    c_all = jnp.concatenate([c_prompt, c_sample], axis=0)
    mod_mix, mod_ffn = _adaln(c_all, w_ada_mix[l], b_ada_mix[l][None, :],
                              w_ada_ffn[l], b_ada_ffn[l][None, :])
    mod_mix = mod_mix[:, None, :]
    mod_ffn = mod_ffn[:, None, :]

    row = lambda a: a[l][None, :]
    params = (row(g_pre_mix), row(g_post_mix), w_in[l].astype(_BF16), w_grp[l].astype(_BF16),
              row(pool_scale), w_pool_proj[l].astype(_BF16), w_dw[l], row(b_dw), row(ln_g),
              row(ln_b), w_conv_proj[l].astype(_BF16), w_out[l].astype(_BF16),
              row(g_pre_ffn), row(g_post_ffn), w_ff1[l].astype(_BF16), w_ff2[l].astype(_BF16))

    y_p, pool_p, conv_p = _layer(x_prompt, mod_mix[:bp], mod_ffn[:bp], None, 0, params)
    y_s, pool_s, conv_s = _layer(x_sample, mod_mix[bp:bp + bs], mod_ffn[bp:bp + bs],
                                 (state_pool[l], state_conv[l]), PAST_LEN, params)
    return (y_p, y_s, pool_p[None], conv_p[None], pool_s[None], conv_s[None])
```

```python
import functools

import jax
import jax.numpy as jnp
from jax import lax
from jax.experimental import pallas as pl
from jax.experimental.pallas import tpu as pltpu

D_MODEL = 1024
POOL_WINDOWS = (2, 4, 8, 16)
POOL_GROUP = 128
D_POOL = POOL_GROUP * len(POOL_WINDOWS)
POOL_HIST = max(POOL_WINDOWS) - 1
D_CONV = 512
CONV_WIDTH = 31
CONV_HIST = CONV_WIDTH - 1
D_IN = D_POOL + 2 * D_CONV + 2 * D_MODEL
D_FF = 4 * D_MODEL
EPS = 1e-6
PAST_LEN = 1024

LANE = 128
SUBLANE = 8
POOL_PAD = 16
CONV_PAD = 32
CONV_ROWS = 32
FF_CHUNK = 1024
FF_CHUNKS_PER_ITER = 2
ADA_TILE = 512
VMEM_LIMIT_BYTES = 56 * 1024 * 1024

_F32 = jnp.float32
_BF16 = jnp.bfloat16


def _dot(a, b):
    return jnp.dot(a, b, preferred_element_type=_F32)


def _sigmoid(x):
    return 1.0 / (1.0 + jnp.exp(-x))


def _rms(x, g):
    return x * lax.rsqrt(jnp.mean(x * x, axis=-1, keepdims=True) + EPS) * g


def _adaln_body(c_ref, wm_ref, bm_ref, wf_ref, bf_ref, om_ref, of_ref):
    c = c_ref[...]
    s = (c * _sigmoid(c)).astype(_BF16)
    om_ref[...] = _dot(s, wm_ref[...].astype(_BF16)) + bm_ref[...]
    of_ref[...] = _dot(s, wf_ref[...].astype(_BF16)) + bf_ref[...]


def _adaln(c, w_mix, b_mix, w_ffn, b_ffn):
    n = c.shape[0]
    n_out = w_mix.shape[1]
    w_spec = pl.BlockSpec((D_MODEL, ADA_TILE), lambda j: (0, j))
    b_spec = pl.BlockSpec((1, ADA_TILE), lambda j: (0, j))
    o_spec = pl.BlockSpec((n, ADA_TILE), lambda j: (0, j))
    return pl.pallas_call(
        _adaln_body,
        grid=(n_out // ADA_TILE,),
        in_specs=[pl.BlockSpec((n, D_MODEL), lambda j: (0, 0)), w_spec, b_spec, w_spec, b_spec],
        out_specs=[o_spec, o_spec],
        out_shape=[jax.ShapeDtypeStruct((n, n_out), _F32)] * 2,
        name="adaln",
    )(c, w_mix, b_mix, w_ffn, b_ffn)


def _lane_major(w):
    k, n = w.shape
    return w.reshape(k, n // LANE, LANE).transpose(1, 0, 2)


def _cols(w_ref, first_tile, n_tiles, rows=None):
    if rows is None:
        parts = [w_ref[first_tile + q] for q in range(n_tiles)]
    else:
        parts = [w_ref[first_tile + q, rows, :] for q in range(n_tiles)]
    return jnp.concatenate(parts, axis=1)


def _layer_body(*refs, bt, tt, nt, n_tiles, pos0, has_state):
    refs = list(refs)
    x_ref, mm_a_ref, mm_b_ref, mf_b_ref = refs[:4]
    refs = refs[4:]
    if has_state:
        sp_ref, sc_ref = refs[:2]
        refs = refs[2:]
    (g_pre_mix, g_post_mix, w_in, w_grp, pool_scale, w_pool_proj, w_dw, b_dw, ln_g, ln_b,
     w_conv_proj, w_out, g_pre_ffn, g_post_ffn, w_ff1, w_ff2,
     y_ref, npool_ref, nconv_ref,
     ubuf, vbuf, zbuf, x_carry, merged_carry, h_buf, h2_buf, za_buf, f_buf, wdw_b) = refs

    m = bt * tt
    step = pl.program_id(0)
    tile_a = jnp.minimum(step, n_tiles - 1)
    t_idx = tile_a % nt
    d_tiles = D_MODEL // LANE

    @pl.when(step == 0)
    def _init_carry():
        x_carry[...] = jnp.zeros((bt, tt, D_MODEL), _F32)
        merged_carry[...] = jnp.zeros((m, D_MODEL), _BF16)
        vbuf[:, CONV_PAD + tt:CONV_PAD + tt + SUBLANE, :] = jnp.zeros((bt, SUBLANE, D_CONV), _F32)
        for j in range(CONV_WIDTH):
            wdw_b[j] = jnp.broadcast_to(w_dw[j:j + 1, :], (SUBLANE, D_CONV))

    @pl.when(t_idx == 0)
    def _init_history():
        ubuf[:, 0:POOL_PAD, :] = jnp.zeros((bt, POOL_PAD, D_POOL), _F32)
        vbuf[:, 0:CONV_PAD, :] = jnp.zeros((bt, CONV_PAD, D_CONV), _F32)
        if has_state:
            ubuf[:, POOL_PAD - POOL_HIST:POOL_PAD, :] = sp_ref[...]
            vbuf[:, CONV_PAD - CONV_HIST:CONV_PAD, :] = sc_ref[...]

    def split3(mod_ref):
        mod = mod_ref[...]
        return (mod[:, :, 0:D_MODEL], mod[:, :, D_MODEL:2 * D_MODEL],
                mod[:, :, 2 * D_MODEL:3 * D_MODEL])

    out = _dot(merged_carry[...], _cols(w_out, 0, d_tiles)).reshape(bt, tt, D_MODEL)
    x = x_ref[...]
    shift, scale, _ = split3(mm_a_ref)
    h = (_rms(x, g_pre_mix[...]) * (1.0 + scale) + shift).astype(_BF16).reshape(m, D_MODEL)
    h_buf[...] = h

    u = _dot(h, w_in[:, 0:D_POOL])
    ubuf[:, POOL_PAD:POOL_PAD + tt, :] = u.reshape(bt, tt, D_POOL)
    v = _dot(h, w_in[:, D_POOL:D_POOL + D_CONV]) * _sigmoid(
        _dot(h, w_in[:, D_POOL + D_CONV:D_POOL + 2 * D_CONV]))
    vbuf[:, CONV_PAD:CONV_PAD + tt, :] = v.reshape(bt, tt, D_CONV)
    _, _, gate_mix = split3(mm_b_ref)
    x1 = x_carry[...] + gate_mix * _rms(out, g_post_mix[...])
    y_ref[...] = x1
    shift, scale, _ = split3(mf_b_ref)
    h2_buf[...] = (_rms(x1, g_pre_ffn[...]) * (1.0 + scale) + shift).astype(_BF16).reshape(
        m, D_MODEL)

    pos = pos0 + t_idx * tt + lax.broadcasted_iota(jnp.int32, (tt, POOL_GROUP), 0)
    for g, k in enumerate(POOL_WINDOWS):
        lo, hi = g * POOL_GROUP, (g + 1) * POOL_GROUP
        cur = ubuf[:, POOL_PAD:POOL_PAD + tt, lo:hi]
        s = cur
        for j in range(1, k):
            s = s + ubuf[:, POOL_PAD - j:POOL_PAD - j + tt, lo:hi]
        cnt = jnp.minimum(k, pos + 1).astype(_F32)
        d = (s / cnt - cur).astype(_BF16).reshape(m, POOL_GROUP)
        za_buf[:, lo:hi] = (_dot(d, w_grp[g]) * pool_scale[:, lo:hi]).astype(_BF16)

    n_ff = D_FF // FF_CHUNK
    n_iter = n_ff // FF_CHUNKS_PER_ITER
    ff_tiles = FF_CHUNK // LANE
    chunks_per_b = tt // CONV_ROWS
    conv_per_iter = bt * chunks_per_b // n_iter
    conv_base = CONV_PAD - CONV_HIST
    n_blk = CONV_ROWS // SUBLANE + 1
    span = CONV_ROWS + CONV_PAD + SUBLANE
    taps_by_shift = [[q - conv_base for q in range(conv_base, conv_base + CONV_WIDTH)
                      if q % SUBLANE == r] for r in range(SUBLANE)]

    def conv_chunk(q):
        b = q // chunks_per_b
        r0 = pl.multiple_of((q % chunks_per_b) * CONV_ROWS, CONV_ROWS)
        acc_tiles = []
        for lt in range(D_CONV // LANE):
            lanes = slice(lt * LANE, (lt + 1) * LANE)
            data = vbuf[b, pl.ds(r0, span), lanes]
            total = None
            for r in range(SUBLANE):
                part = None
                for j in taps_by_shift[r]:
                    blk = (j + conv_base) // SUBLANE
                    w_rep = jnp.concatenate([wdw_b[j, :, lanes]] * n_blk, axis=0)
                    term = w_rep * data[blk * SUBLANE:(blk + n_blk) * SUBLANE]
                    part = term if part is None else part + term
                shifted = part[r:r + CONV_ROWS]
                total = shifted if total is None else total + shifted
            acc_tiles.append(total)
        acc = jnp.concatenate(acc_tiles, axis=1) + b_dw[...]
        mu = jnp.mean(acc, axis=-1, keepdims=True)
        xc = acc - mu
        var = jnp.mean(xc * xc, axis=-1, keepdims=True)
        ln = xc * lax.rsqrt(var + EPS) * ln_g[...] + ln_b[...]
        row0 = pl.multiple_of(b * tt + r0, CONV_ROWS)
        zbuf[pl.ds(row0, CONV_ROWS), :] = (ln * _sigmoid(ln)).astype(_BF16)

    f_buf[...] = jnp.zeros((m, D_MODEL), _F32)

    def middle(it, carry):
        h2 = h2_buf[...]
        acts = []
        for c in range(FF_CHUNKS_PER_ITER):
            first = (it * FF_CHUNKS_PER_ITER + c) * ff_tiles
            a = jnp.maximum(_dot(h2, _cols(w_ff1, first, ff_tiles)), 0.0)
            acts.append((a * a).astype(_BF16))
        f = f_buf[...]
        for c in range(FF_CHUNKS_PER_ITER):
            rows = pl.ds(pl.multiple_of((it * FF_CHUNKS_PER_ITER + c) * FF_CHUNK, FF_CHUNK),
                         FF_CHUNK)
            f = f + _dot(acts[c], _cols(w_ff2, 0, d_tiles, rows))
        f_buf[...] = f
        for c in range(conv_per_iter):
            conv_chunk(it * conv_per_iter + c)
        return carry

    lax.fori_loop(0, n_iter, middle, 0)

    y_a = _dot(za_buf[...], _cols(w_pool_proj, 0, d_tiles))
    _, _, gate_ffn = split3(mf_b_ref)
    y_ref[...] = y_ref[...] + gate_ffn * _rms(f_buf[...].reshape(bt, tt, D_MODEL), g_post_ffn[...])
    y_b = _dot(zbuf[...], _cols(w_conv_proj, 0, d_tiles))
    off = D_POOL + 2 * D_CONV
    h = h_buf[...]
    g_a = _sigmoid(_dot(h, w_in[:, off:off + D_MODEL]))
    g_b = _sigmoid(_dot(h, w_in[:, off + D_MODEL:off + 2 * D_MODEL]))
    merged_carry[...] = (g_a * y_a + g_b * y_b).astype(_BF16)
    x_carry[...] = x_ref[...]

    @pl.when(jnp.logical_and(t_idx == nt - 1, step < n_tiles))
    def _write_state():
        npool_ref[...] = ubuf[:, POOL_PAD + tt - POOL_HIST:POOL_PAD + tt, :]
        nconv_ref[...] = vbuf[:, CONV_PAD + tt - CONV_HIST:CONV_PAD + tt, :]

    ubuf[:, 0:POOL_PAD, :] = ubuf[:, tt:tt + POOL_PAD, :]
    vbuf[:, 0:CONV_PAD, :] = vbuf[:, tt:tt + CONV_PAD, :]


def _tile_rows(batch, seq):
    rows = 256
    tt = min(seq, rows)
    bt = max(1, min(batch, rows // tt))
    return bt, tt


def _layer(x, mod_mix, mod_ffn, state, pos0, params):
    batch, seq, _ = x.shape
    bt, tt = _tile_rows(batch, seq)
    n_iter = D_FF // FF_CHUNK // FF_CHUNKS_PER_ITER
    assert batch % bt == 0 and seq % tt == 0 and tt % CONV_ROWS == 0 and tt >= CONV_PAD
    assert (bt * tt // CONV_ROWS) % n_iter == 0
    has_state = state is not None
    m = bt * tt
    nt = seq // tt
    n_tiles = (batch // bt) * nt

    def tile_a(s):
        return jnp.minimum(s, n_tiles - 1)

    def tile_b(s):
        return jnp.maximum(s - 1, 0)

    def per_batch(rows, cols, tile):
        return pl.BlockSpec((bt, rows, cols), lambda s: (tile(s) // nt, 0, 0))

    def per_tile(tile):
        return pl.BlockSpec((bt, tt, D_MODEL), lambda s: (tile(s) // nt, tile(s) % nt, 0))

    def whole(a):
        nd = a.ndim
        return pl.BlockSpec(a.shape, lambda s: (0,) * nd, pipeline_mode=pl.Buffered(1))

    in_specs = [per_tile(tile_a), per_batch(1, 3 * D_MODEL, tile_a),
                per_batch(1, 3 * D_MODEL, tile_b), per_batch(1, 3 * D_MODEL, tile_b)]
    args = [x, mod_mix, mod_mix, mod_ffn]
    if has_state:
        in_specs += [per_batch(POOL_HIST, D_POOL, tile_a), per_batch(CONV_HIST, D_CONV, tile_a)]
        args += list(state)
    in_specs += [whole(p) for p in params]
    args += list(params)

    return pl.pallas_call(
        functools.partial(_layer_body, bt=bt, tt=tt, nt=nt, n_tiles=n_tiles, pos0=pos0,
                          has_state=has_state),
        grid=(n_tiles + 1,),
        in_specs=in_specs,
        out_specs=[per_tile(tile_b), per_batch(POOL_HIST, D_POOL, tile_a),
                   per_batch(CONV_HIST, D_CONV, tile_a)],
        out_shape=[jax.ShapeDtypeStruct(x.shape, _F32),
                   jax.ShapeDtypeStruct((batch, POOL_HIST, D_POOL), _F32),
                   jax.ShapeDtypeStruct((batch, CONV_HIST, D_CONV), _F32)],
        scratch_shapes=[pltpu.VMEM((bt, POOL_PAD + tt, D_POOL), _F32),
                        pltpu.VMEM((bt, CONV_PAD + tt + SUBLANE, D_CONV), _F32),
                        pltpu.VMEM((m, D_CONV), _BF16),
                        pltpu.VMEM((bt, tt, D_MODEL), _F32),
                        pltpu.VMEM((m, D_MODEL), _BF16),
                        pltpu.VMEM((m, D_MODEL), _BF16),
                        pltpu.VMEM((m, D_MODEL), _BF16),
                        pltpu.VMEM((m, D_POOL), _BF16),
                        pltpu.VMEM((m, D_MODEL), _F32),
                        pltpu.VMEM((CONV_WIDTH, SUBLANE, D_CONV), _F32)],
        compiler_params=pltpu.CompilerParams(
            dimension_semantics=("arbitrary",),
            vmem_limit_bytes=VMEM_LIMIT_BYTES),
        name="layer_state" if has_state else "layer",
    )(*args)


def kernel(x_prompt, x_sample, state_pool, state_conv, c_prompt, c_sample, w_ada_mix, b_ada_mix, g_pre_mix, g_post_mix, w_in, w_grp, pool_scale, w_pool_proj, w_dw, b_dw, ln_g, ln_b, w_conv_proj, w_out, w_ada_ffn, b_ada_ffn, g_pre_ffn, g_post_ffn, w_ff1, w_ff2):
    depth = w_in.shape[0]
    assert depth == 1
    l = 0
    bp = x_prompt.shape[0]
    bs = x_sample.shape[0]

    c_all = jnp.concatenate([c_prompt, c_sample], axis=0)
    mod_mix, mod_ffn = _adaln(c_all, w_ada_mix[l], b_ada_mix[l][None, :],
                              w_ada_ffn[l], b_ada_ffn[l][None, :])
    mod_mix = mod_mix[:, None, :]
    mod_ffn = mod_ffn[:, None, :]

    row = lambda a: a[l][None, :]
    params = (row(g_pre_mix), row(g_post_mix), w_in[l].astype(_BF16), w_grp[l].astype(_BF16),
              row(pool_scale), _lane_major(w_pool_proj[l].astype(_BF16)), w_dw[l], row(b_dw),
              row(ln_g), row(ln_b), _lane_major(w_conv_proj[l].astype(_BF16)),
              _lane_major(w_out[l].astype(_BF16)), row(g_pre_ffn), row(g_post_ffn),
              _lane_major(w_ff1[l].astype(_BF16)), _lane_major(w_ff2[l].astype(_BF16)))

    y_p, pool_p, conv_p = _layer(x_prompt, mod_mix[:bp], mod_ffn[:bp], None, 0, params)
    y_s, pool_s, conv_s = _layer(x_sample, mod_mix[bp:bp + bs], mod_ffn[bp:bp + bs],
                                 (state_pool[l], state_conv[l]), PAST_LEN, params)
    return (y_p, y_s, pool_p[None], conv_p[None], pool_s[None], conv_s[None])
```

```python
import functools

import jax
import jax.numpy as jnp
from jax import lax
from jax.experimental import pallas as pl
from jax.experimental.pallas import tpu as pltpu

D_MODEL = 1024
POOL_WINDOWS = (2, 4, 8, 16)
POOL_GROUP = 128
D_POOL = POOL_GROUP * len(POOL_WINDOWS)
POOL_HIST = max(POOL_WINDOWS) - 1
D_CONV = 512
CONV_WIDTH = 31
CONV_HIST = CONV_WIDTH - 1
D_IN = D_POOL + 2 * D_CONV + 2 * D_MODEL
D_FF = 4 * D_MODEL
EPS = 1e-6
PAST_LEN = 1024

LANE = 128
SUBLANE = 8
POOL_PAD = 16
CONV_PAD = 32
ROW_STRIDE = 4
CONV_ROWS = ROW_STRIDE * SUBLANE
FF_CHUNK = 1024
FF_CHUNKS_PER_ITER = 2
ADA_TILE = 512
VMEM_LIMIT_BYTES = 56 * 1024 * 1024

_F32 = jnp.float32
_BF16 = jnp.bfloat16


def _dot(a, b):
    return jnp.dot(a, b, preferred_element_type=_F32)


def _sigmoid(x):
    return 1.0 / (1.0 + jnp.exp(-x))


def _rms(x, g):
    return x * lax.rsqrt(jnp.mean(x * x, axis=-1, keepdims=True) + EPS) * g


def _adaln_body(c_ref, wm_ref, bm_ref, wf_ref, bf_ref, om_ref, of_ref):
    c = c_ref[...]
    s = (c * _sigmoid(c)).astype(_BF16)
    om_ref[...] = _dot(s, wm_ref[...].astype(_BF16)) + bm_ref[...]
    of_ref[...] = _dot(s, wf_ref[...].astype(_BF16)) + bf_ref[...]


def _adaln(c, w_mix, b_mix, w_ffn, b_ffn):
    n = c.shape[0]
    n_out = w_mix.shape[1]
    w_spec = pl.BlockSpec((D_MODEL, ADA_TILE), lambda j: (0, j))
    b_spec = pl.BlockSpec((1, ADA_TILE), lambda j: (0, j))
    o_spec = pl.BlockSpec((n, ADA_TILE), lambda j: (0, j))
    return pl.pallas_call(
        _adaln_body,
        grid=(n_out // ADA_TILE,),
        in_specs=[pl.BlockSpec((n, D_MODEL), lambda j: (0, 0)), w_spec, b_spec, w_spec, b_spec],
        out_specs=[o_spec, o_spec],
        out_shape=[jax.ShapeDtypeStruct((n, n_out), _F32)] * 2,
        name="adaln",
    )(c, w_mix, b_mix, w_ffn, b_ffn)


def _lane_major(w):
    k, n = w.shape
    return w.reshape(k, n // LANE, LANE).transpose(1, 0, 2)


def _cols(w_ref, first_tile, n_tiles, rows=None):
    if rows is None:
        parts = [w_ref[first_tile + q] for q in range(n_tiles)]
    else:
        parts = [w_ref[first_tile + q, rows, :] for q in range(n_tiles)]
    return jnp.concatenate(parts, axis=1)


def _layer_body(*refs, bt, tt, nt, n_tiles, pos0, has_state):
    refs = list(refs)
    x_ref, mm_a_ref, mm_b_ref, mf_b_ref = refs[:4]
    refs = refs[4:]
    if has_state:
        sp_ref, sc_ref = refs[:2]
        refs = refs[2:]
    (g_pre_mix, g_post_mix, w_in, w_grp, pool_scale, w_pool_proj, w_dw, b_dw, ln_g, ln_b,
     w_conv_proj, w_out, g_pre_ffn, g_post_ffn, w_ff1, w_ff2,
     y_ref, npool_ref, nconv_ref,
     ubuf, vbuf, x_carry, merged_carry, h_buf, h2_buf, za_buf, f_buf, wdw_b, d_buf) = refs

    m = bt * tt
    step = pl.program_id(0)
    tile_a = jnp.minimum(step, n_tiles - 1)
    t_idx = tile_a % nt
    d_tiles = D_MODEL // LANE
    pool_tiles = D_POOL // LANE
    conv_tiles = D_CONV // LANE

    @pl.when(step == 0)
    def _init_carry():
        x_carry[...] = jnp.zeros((bt, tt, D_MODEL), _F32)
        merged_carry[...] = jnp.zeros((m, D_MODEL), _BF16)
        for j in range(CONV_WIDTH):
            wdw_b[j] = jnp.broadcast_to(w_dw[j:j + 1, :], (SUBLANE, D_CONV))

    @pl.when(t_idx == 0)
    def _init_history():
        ubuf[:, 0:POOL_PAD, :] = jnp.zeros((bt * pool_tiles, POOL_PAD, LANE), _F32)
        vbuf[0:bt * conv_tiles, 0:CONV_PAD, :] = jnp.zeros((bt * conv_tiles, CONV_PAD, LANE), _F32)
        if has_state:
            for b in range(bt):
                for lt in range(pool_tiles):
                    ubuf[b * pool_tiles + lt, POOL_PAD - POOL_HIST:POOL_PAD, :] = (
                        sp_ref[b, :, lt * LANE:(lt + 1) * LANE])
                for lt in range(conv_tiles):
                    vbuf[b * conv_tiles + lt, CONV_PAD - CONV_HIST:CONV_PAD, :] = (
                        sc_ref[b, :, lt * LANE:(lt + 1) * LANE])

    def split3(mod_ref):
        mod = mod_ref[...]
        return (mod[:, :, 0:D_MODEL], mod[:, :, D_MODEL:2 * D_MODEL],
                mod[:, :, 2 * D_MODEL:3 * D_MODEL])

    out = _dot(merged_carry[...], _cols(w_out, 0, d_tiles)).reshape(bt, tt, D_MODEL)
    x = x_ref[...]
    shift, scale, _ = split3(mm_a_ref)
    h = (_rms(x, g_pre_mix[...]) * (1.0 + scale) + shift).astype(_BF16).reshape(m, D_MODEL)
    h_buf[...] = h

    u = _dot(h, w_in[:, 0:D_POOL])
    for b in range(bt):
        for lt in range(pool_tiles):
            ubuf[b * pool_tiles + lt, POOL_PAD:POOL_PAD + tt, :] = (
                u[b * tt:(b + 1) * tt, lt * LANE:(lt + 1) * LANE])
    v = _dot(h, w_in[:, D_POOL:D_POOL + D_CONV]) * _sigmoid(
        _dot(h, w_in[:, D_POOL + D_CONV:D_POOL + 2 * D_CONV]))
    for b in range(bt):
        for lt in range(conv_tiles):
            vbuf[b * conv_tiles + lt, CONV_PAD:CONV_PAD + tt, :] = (
                v[b * tt:(b + 1) * tt, lt * LANE:(lt + 1) * LANE])
    _, _, gate_mix = split3(mm_b_ref)
    x1 = x_carry[...] + gate_mix * _rms(out, g_post_mix[...])
    y_ref[...] = x1
    shift, scale, _ = split3(mf_b_ref)
    h2_buf[...] = (_rms(x1, g_pre_ffn[...]) * (1.0 + scale) + shift).astype(_BF16).reshape(
        m, D_MODEL)

    pos_base = pos0 + t_idx * tt
    row_iota = ROW_STRIDE * lax.broadcasted_iota(jnp.int32, (SUBLANE, LANE), 0)
    for g, k in enumerate(POOL_WINDOWS):
        for b in range(bt):
            slab = b * pool_tiles + g
            for r0 in range(0, tt, CONV_ROWS):
                for rho in range(ROW_STRIDE):
                    first = POOL_PAD + r0 + rho
                    cur = ubuf[slab, pl.ds(first, SUBLANE, stride=ROW_STRIDE), :]
                    s = cur
                    for j in range(1, k):
                        s = s + ubuf[slab, pl.ds(first - j, SUBLANE, stride=ROW_STRIDE), :]
                    cnt = jnp.minimum(k, pos_base + (r0 + rho + 1) + row_iota).astype(_F32)
                    d_buf[g, pl.ds(b * tt + r0 + rho, SUBLANE, stride=ROW_STRIDE), :] = (
                        s / cnt - cur)
        lo, hi = g * POOL_GROUP, (g + 1) * POOL_GROUP
        za_buf[:, lo:hi] = (_dot(d_buf[g].astype(_BF16), w_grp[g])
                            * pool_scale[:, lo:hi]).astype(_BF16)

    n_ff = D_FF // FF_CHUNK
    n_iter = n_ff // FF_CHUNKS_PER_ITER
    ff_tiles = FF_CHUNK // LANE
    chunks_per_b = tt // CONV_ROWS
    conv_per_iter = bt * chunks_per_b // n_iter
    conv_base = CONV_PAD - CONV_HIST

    def conv_chunk(q):
        b = q // chunks_per_b
        r0 = (q % chunks_per_b) * CONV_ROWS
        for lt in range(conv_tiles):
            lanes = slice(lt * LANE, (lt + 1) * LANE)
            slab = b * conv_tiles + lt
            accs = [None] * ROW_STRIDE
            for o in range(CONV_WIDTH + ROW_STRIDE - 1):
                d_o = vbuf[slab, pl.ds(r0 + (conv_base + o), SUBLANE, stride=ROW_STRIDE), :]
                for rho in range(ROW_STRIDE):
                    j = o - rho
                    if 0 <= j < CONV_WIDTH:
                        term = wdw_b[j, :, lanes] * d_o
                        accs[rho] = term if accs[rho] is None else accs[rho] + term
            for rho in range(ROW_STRIDE):
                vbuf[bt * conv_tiles + slab, pl.ds(r0 + rho, SUBLANE, stride=ROW_STRIDE), :] = (
                    accs[rho] + b_dw[:, lanes])

    f_buf[...] = jnp.zeros((m, D_MODEL), _F32)

    def middle(it, carry):
        h2 = h2_buf[...]
        acts = []
        for c in range(FF_CHUNKS_PER_ITER):
            first = (it * FF_CHUNKS_PER_ITER + c) * ff_tiles
            a = jnp.maximum(_dot(h2, _cols(w_ff1, first, ff_tiles)), 0.0)
            acts.append((a * a).astype(_BF16))
        f = f_buf[...]
        for c in range(FF_CHUNKS_PER_ITER):
            rows = pl.ds(pl.multiple_of((it * FF_CHUNKS_PER_ITER + c) * FF_CHUNK, FF_CHUNK),
                         FF_CHUNK)
            f = f + _dot(acts[c], _cols(w_ff2, 0, d_tiles, rows))
        f_buf[...] = f
        for c in range(conv_per_iter):
            conv_chunk(it * conv_per_iter + c)
        return carry

    lax.fori_loop(0, n_iter, middle, 0)

    y_a = _dot(za_buf[...], _cols(w_pool_proj, 0, d_tiles))
    _, _, gate_ffn = split3(mf_b_ref)
    y_ref[...] = y_ref[...] + gate_ffn * _rms(f_buf[...].reshape(bt, tt, D_MODEL), g_post_ffn[...])
    conv = jnp.concatenate(
        [jnp.concatenate([vbuf[(bt + b) * conv_tiles + lt, 0:tt, :]
                          for lt in range(conv_tiles)], axis=1) for b in range(bt)], axis=0)
    mu = jnp.mean(conv, axis=-1, keepdims=True)
    xc = conv - mu
    var = jnp.mean(xc * xc, axis=-1, keepdims=True)
    ln = xc * lax.rsqrt(var + EPS) * ln_g[...] + ln_b[...]
    z_b = (ln * _sigmoid(ln)).astype(_BF16)
    off = D_POOL + 2 * D_CONV
    h = h_buf[...]
    g_a = _sigmoid(_dot(h, w_in[:, off:off + D_MODEL]))
    g_b = _sigmoid(_dot(h, w_in[:, off + D_MODEL:off + 2 * D_MODEL]))
    y_b = _dot(z_b, _cols(w_conv_proj, 0, d_tiles))
    merged_carry[...] = (g_a * y_a + g_b * y_b).astype(_BF16)
    x_carry[...] = x_ref[...]

    @pl.when(jnp.logical_and(t_idx == nt - 1, step < n_tiles))
    def _write_state():
        for b in range(bt):
            for lt in range(pool_tiles):
                npool_ref[b, :, lt * LANE:(lt + 1) * LANE] = (
                    ubuf[b * pool_tiles + lt, POOL_PAD + tt - POOL_HIST:POOL_PAD + tt, :])
            for lt in range(conv_tiles):
                nconv_ref[b, :, lt * LANE:(lt + 1) * LANE] = (
                    vbuf[b * conv_tiles + lt, CONV_PAD + tt - CONV_HIST:CONV_PAD + tt, :])

    ubuf[:, 0:POOL_PAD, :] = ubuf[:, tt:tt + POOL_PAD, :]
    vbuf[0:bt * conv_tiles, 0:CONV_PAD, :] = vbuf[0:bt * conv_tiles, tt:tt + CONV_PAD, :]


def _tile_rows(batch, seq):
    rows = 256
    tt = min(seq, rows)
    bt = max(1, min(batch, rows // tt))
    return bt, tt


def _layer(x, mod_mix, mod_ffn, state, pos0, params):
    batch, seq, _ = x.shape
    bt, tt = _tile_rows(batch, seq)
    n_iter = D_FF // FF_CHUNK // FF_CHUNKS_PER_ITER
    assert batch % bt == 0 and seq % tt == 0 and tt % CONV_ROWS == 0 and tt >= CONV_PAD
    assert (bt * tt // CONV_ROWS) % n_iter == 0
    has_state = state is not None
    m = bt * tt
    nt = seq // tt
    n_tiles = (batch // bt) * nt

    def tile_a(s):
        return jnp.minimum(s, n_tiles - 1)

    def tile_b(s):
        return jnp.maximum(s - 1, 0)

    def per_batch(rows, cols, tile):
        return pl.BlockSpec((bt, rows, cols), lambda s: (tile(s) // nt, 0, 0))

    def per_tile(tile):
        return pl.BlockSpec((bt, tt, D_MODEL), lambda s: (tile(s) // nt, tile(s) % nt, 0))

    def whole(a):
        nd = a.ndim
        return pl.BlockSpec(a.shape, lambda s: (0,) * nd, pipeline_mode=pl.Buffered(1))

    in_specs = [per_tile(tile_a), per_batch(1, 3 * D_MODEL, tile_a),
                per_batch(1, 3 * D_MODEL, tile_b), per_batch(1, 3 * D_MODEL, tile_b)]
    args = [x, mod_mix, mod_mix, mod_ffn]
    if has_state:
        in_specs += [per_batch(POOL_HIST, D_POOL, tile_a), per_batch(CONV_HIST, D_CONV, tile_a)]
        args += list(state)
    in_specs += [whole(p) for p in params]
    args += list(params)

    return pl.pallas_call(
        functools.partial(_layer_body, bt=bt, tt=tt, nt=nt, n_tiles=n_tiles, pos0=pos0,
                          has_state=has_state),
        grid=(n_tiles + 1,),
        in_specs=in_specs,
        out_specs=[per_tile(tile_b), per_batch(POOL_HIST, D_POOL, tile_a),
                   per_batch(CONV_HIST, D_CONV, tile_a)],
        out_shape=[jax.ShapeDtypeStruct(x.shape, _F32),
                   jax.ShapeDtypeStruct((batch, POOL_HIST, D_POOL), _F32),
                   jax.ShapeDtypeStruct((batch, CONV_HIST, D_CONV), _F32)],
        scratch_shapes=[pltpu.VMEM((bt * D_POOL // LANE, POOL_PAD + tt, LANE), _F32),
                        pltpu.VMEM((2 * bt * D_CONV // LANE, CONV_PAD + tt, LANE), _F32),
                        pltpu.VMEM((bt, tt, D_MODEL), _F32),
                        pltpu.VMEM((m, D_MODEL), _BF16),
                        pltpu.VMEM((m, D_MODEL), _BF16),
                        pltpu.VMEM((m, D_MODEL), _BF16),
                        pltpu.VMEM((m, D_POOL), _BF16),
                        pltpu.VMEM((m, D_MODEL), _F32),
                        pltpu.VMEM((CONV_WIDTH, SUBLANE, D_CONV), _F32),
                        pltpu.VMEM((D_POOL // LANE, m, LANE), _F32)],
        compiler_params=pltpu.CompilerParams(
            dimension_semantics=("arbitrary",),
            vmem_limit_bytes=VMEM_LIMIT_BYTES),
        name="layer_state" if has_state else "layer",
    )(*args)


def kernel(x_prompt, x_sample, state_pool, state_conv, c_prompt, c_sample, w_ada_mix, b_ada_mix, g_pre_mix, g_post_mix, w_in, w_grp, pool_scale, w_pool_proj, w_dw, b_dw, ln_g, ln_b, w_conv_proj, w_out, w_ada_ffn, b_ada_ffn, g_pre_ffn, g_post_ffn, w_ff1, w_ff2):
    depth = w_in.shape[0]
    assert depth == 1
    l = 0
    bp = x_prompt.shape[0]
    bs = x_sample.shape[0]

    c_all = jnp.concatenate([c_prompt, c_sample], axis=0)
    mod_mix, mod_ffn = _adaln(c_all, w_ada_mix[l], b_ada_mix[l][None, :],
                              w_ada_ffn[l], b_ada_ffn[l][None, :])
    mod_mix = mod_mix[:, None, :]
    mod_ffn = mod_ffn[:, None, :]

    row = lambda a: a[l][None, :]
    params = (row(g_pre_mix), row(g_post_mix), w_in[l].astype(_BF16), w_grp[l].astype(_BF16),
              row(pool_scale), _lane_major(w_pool_proj[l].astype(_BF16)), w_dw[l], row(b_dw),
              row(ln_g), row(ln_b), _lane_major(w_conv_proj[l].astype(_BF16)),
              _lane_major(w_out[l].astype(_BF16)), row(g_pre_ffn), row(g_post_ffn),
              _lane_major(w_ff1[l].astype(_BF16)), _lane_major(w_ff2[l].astype(_BF16)))

    y_p, pool_p, conv_p = _layer(x_prompt, mod_mix[:bp], mod_ffn[:bp], None, 0, params)
    y_s, pool_s, conv_s = _layer(x_sample, mod_mix[bp:bp + bs], mod_ffn[bp:bp + bs],
                                 (state_pool[l], state_conv[l]), PAST_LEN, params)
    return (y_p, y_s, pool_p[None], conv_p[None], pool_s[None], conv_s[None])
```

```python
import functools

import jax
import jax.numpy as jnp
from jax import lax
from jax.experimental import pallas as pl
from jax.experimental.pallas import tpu as pltpu

D_MODEL = 1024
POOL_WINDOWS = (2, 4, 8, 16)
POOL_GROUP = 128
D_POOL = POOL_GROUP * len(POOL_WINDOWS)
POOL_HIST = max(POOL_WINDOWS) - 1
D_CONV = 512
CONV_WIDTH = 31
CONV_HIST = CONV_WIDTH - 1
D_IN = D_POOL + 2 * D_CONV + 2 * D_MODEL
D_FF = 4 * D_MODEL
EPS = 1e-6
PAST_LEN = 1024

LANE = 128
SUBLANE = 8
POOL_PAD = 16
CONV_PAD = 32
ROW_STRIDE = 4
CONV_ROWS = ROW_STRIDE * SUBLANE
FF_CHUNK = 1024
FF_CHUNKS_PER_ITER = 2
ADA_TILE = 512
VMEM_LIMIT_BYTES = 56 * 1024 * 1024

_F32 = jnp.float32
_BF16 = jnp.bfloat16


def _dot(a, b):
    return jnp.dot(a, b, preferred_element_type=_F32)


def _sigmoid(x):
    return 1.0 / (1.0 + jnp.exp(-x))


def _rms(x, g):
    return x * lax.rsqrt(jnp.mean(x * x, axis=-1, keepdims=True) + EPS) * g


def _adaln_body(c_ref, wm_ref, bm_ref, wf_ref, bf_ref, om_ref, of_ref):
    c = c_ref[...]
    s = (c * _sigmoid(c)).astype(_BF16)
    om_ref[...] = _dot(s, wm_ref[...].astype(_BF16)) + bm_ref[...]
    of_ref[...] = _dot(s, wf_ref[...].astype(_BF16)) + bf_ref[...]


def _adaln(c, w_mix, b_mix, w_ffn, b_ffn):
    n = c.shape[0]
    n_out = w_mix.shape[1]
    w_spec = pl.BlockSpec((D_MODEL, ADA_TILE), lambda j: (0, j))
    b_spec = pl.BlockSpec((1, ADA_TILE), lambda j: (0, j))
    o_spec = pl.BlockSpec((n, ADA_TILE), lambda j: (0, j))
    return pl.pallas_call(
        _adaln_body,
        grid=(n_out // ADA_TILE,),
        in_specs=[pl.BlockSpec((n, D_MODEL), lambda j: (0, 0)), w_spec, b_spec, w_spec, b_spec],
        out_specs=[o_spec, o_spec],
        out_shape=[jax.ShapeDtypeStruct((n, n_out), _F32)] * 2,
        name="adaln",
    )(c, w_mix, b_mix, w_ffn, b_ffn)


def _lane_major(w):
    k, n = w.shape
    return w.reshape(k, n // LANE, LANE).transpose(1, 0, 2)


def _cols(w_ref, first_tile, n_tiles, rows=None):
    if rows is None:
        parts = [w_ref[first_tile + q] for q in range(n_tiles)]
    else:
        parts = [w_ref[first_tile + q, rows, :] for q in range(n_tiles)]
    return jnp.concatenate(parts, axis=1)


def _layer_body(*refs, bt, tt, nt, n_tiles, pos0, has_state):
    refs = list(refs)
    x_ref, mm_a_ref, mm_b_ref, mf_b_ref = refs[:4]
    refs = refs[4:]
    if has_state:
        sp_ref, sc_ref = refs[:2]
        refs = refs[2:]
    (g_pre_mix, g_post_mix, w_in, w_grp, pool_scale, w_pool_proj, w_dw, b_dw, ln_g, ln_b,
     w_conv_proj, w_out, g_pre_ffn, g_post_ffn, w_ff1, w_ff2,
     y_ref, npool_ref, nconv_ref,
     ubuf, vbuf, x_carry, merged_carry, h_buf, h2_buf, za_buf, f_buf, wdw_b, d_buf) = refs

    m = bt * tt
    step = pl.program_id(0)
    tile_a = jnp.minimum(step, n_tiles - 1)
    t_idx = tile_a % nt
    d_tiles = D_MODEL // LANE
    pool_tiles = D_POOL // LANE
    conv_tiles = D_CONV // LANE

    @pl.when(step == 0)
    def _init_carry():
        x_carry[...] = jnp.zeros((bt, tt, D_MODEL), _F32)
        merged_carry[...] = jnp.zeros((m, D_MODEL), _BF16)
        vbuf[:, CONV_PAD + tt:CONV_PAD + tt + SUBLANE, :] = jnp.zeros(
            (2 * bt * conv_tiles, SUBLANE, LANE), _F32)
        for j in range(CONV_WIDTH):
            wdw_b[j] = jnp.broadcast_to(w_dw[j:j + 1, :], (SUBLANE, D_CONV))

    @pl.when(t_idx == 0)
    def _init_history():
        ubuf[:, 0:POOL_PAD, :] = jnp.zeros((bt * pool_tiles, POOL_PAD, LANE), _F32)
        vbuf[0:bt * conv_tiles, 0:CONV_PAD, :] = jnp.zeros((bt * conv_tiles, CONV_PAD, LANE), _F32)
        if has_state:
            for b in range(bt):
                for lt in range(pool_tiles):
                    ubuf[b * pool_tiles + lt, POOL_PAD - POOL_HIST:POOL_PAD, :] = (
                        sp_ref[b, :, lt * LANE:(lt + 1) * LANE])
                for lt in range(conv_tiles):
                    vbuf[b * conv_tiles + lt, CONV_PAD - CONV_HIST:CONV_PAD, :] = (
                        sc_ref[b, :, lt * LANE:(lt + 1) * LANE])

    def split3(mod_ref):
        mod = mod_ref[...]
        return (mod[:, :, 0:D_MODEL], mod[:, :, D_MODEL:2 * D_MODEL],
                mod[:, :, 2 * D_MODEL:3 * D_MODEL])

    out = _dot(merged_carry[...], _cols(w_out, 0, d_tiles)).reshape(bt, tt, D_MODEL)
    x = x_ref[...]
    shift, scale, _ = split3(mm_a_ref)
    h = (_rms(x, g_pre_mix[...]) * (1.0 + scale) + shift).astype(_BF16).reshape(m, D_MODEL)
    h_buf[...] = h

    u = _dot(h, w_in[:, 0:D_POOL])
    for b in range(bt):
        for lt in range(pool_tiles):
            ubuf[b * pool_tiles + lt, POOL_PAD:POOL_PAD + tt, :] = (
                u[b * tt:(b + 1) * tt, lt * LANE:(lt + 1) * LANE])
    v = _dot(h, w_in[:, D_POOL:D_POOL + D_CONV]) * _sigmoid(
        _dot(h, w_in[:, D_POOL + D_CONV:D_POOL + 2 * D_CONV]))
    for b in range(bt):
        for lt in range(conv_tiles):
            vbuf[b * conv_tiles + lt, CONV_PAD:CONV_PAD + tt, :] = (
                v[b * tt:(b + 1) * tt, lt * LANE:(lt + 1) * LANE])
    _, _, gate_mix = split3(mm_b_ref)
    x1 = x_carry[...] + gate_mix * _rms(out, g_post_mix[...])
    y_ref[...] = x1
    shift, scale, _ = split3(mf_b_ref)
    h2_buf[...] = (_rms(x1, g_pre_ffn[...]) * (1.0 + scale) + shift).astype(_BF16).reshape(
        m, D_MODEL)

    pos_base = pos0 + t_idx * tt
    row_iota = ROW_STRIDE * lax.broadcasted_iota(jnp.int32, (SUBLANE, LANE), 0)
    for g, k in enumerate(POOL_WINDOWS):
        for b in range(bt):
            slab = b * pool_tiles + g
            for r0 in range(0, tt, CONV_ROWS):
                for rho in range(ROW_STRIDE):
                    first = POOL_PAD + r0 + rho
                    cur = ubuf[slab, pl.ds(first, SUBLANE, stride=ROW_STRIDE), :]
                    s = cur
                    for j in range(1, k):
                        s = s + ubuf[slab, pl.ds(first - j, SUBLANE, stride=ROW_STRIDE), :]
                    cnt = jnp.minimum(k, pos_base + (r0 + rho + 1) + row_iota).astype(_F32)
                    d_buf[g, pl.ds(b * tt + r0 + rho, SUBLANE, stride=ROW_STRIDE), :] = (
                        s / cnt - cur)
        lo, hi = g * POOL_GROUP, (g + 1) * POOL_GROUP
        za_buf[:, lo:hi] = (_dot(d_buf[g].astype(_BF16), w_grp[g])
                            * pool_scale[:, lo:hi]).astype(_BF16)

    n_ff = D_FF // FF_CHUNK
    n_iter = n_ff // FF_CHUNKS_PER_ITER
    ff_tiles = FF_CHUNK // LANE
    chunks_per_b = tt // CONV_ROWS
    conv_per_iter = bt * chunks_per_b // n_iter
    conv_base = CONV_PAD - CONV_HIST
    n_blk = CONV_ROWS // SUBLANE + 1
    span = CONV_ROWS + CONV_PAD + SUBLANE
    taps_by_shift = [[q - conv_base for q in range(conv_base, conv_base + CONV_WIDTH)
                      if q % SUBLANE == r] for r in range(SUBLANE)]

    def conv_chunk(q):
        b = q // chunks_per_b
        r0 = pl.multiple_of((q % chunks_per_b) * CONV_ROWS, CONV_ROWS)
        for lt in range(conv_tiles):
            lanes = slice(lt * LANE, (lt + 1) * LANE)
            slab = b * conv_tiles + lt
            data = vbuf[slab, pl.ds(r0, span), :]
            total = None
            for r in range(SUBLANE):
                part = None
                for j in taps_by_shift[r]:
                    blk = (j + conv_base) // SUBLANE
                    w_rep = jnp.concatenate([wdw_b[j, :, lanes]] * n_blk, axis=0)
                    term = w_rep * data[blk * SUBLANE:(blk + n_blk) * SUBLANE]
                    part = term if part is None else part + term
                shifted = part[r:r + CONV_ROWS]
                total = shifted if total is None else total + shifted
            vbuf[bt * conv_tiles + slab, pl.ds(r0, CONV_ROWS), :] = total + b_dw[:, lanes]

    f_buf[...] = jnp.zeros((m, D_MODEL), _F32)

    def middle(it, carry):
        h2 = h2_buf[...]
        acts = []
        for c in range(FF_CHUNKS_PER_ITER):
            first = (it * FF_CHUNKS_PER_ITER + c) * ff_tiles
            a = jnp.maximum(_dot(h2, _cols(w_ff1, first, ff_tiles)), 0.0)
            acts.append((a * a).astype(_BF16))
        f = f_buf[...]
        for c in range(FF_CHUNKS_PER_ITER):
            rows = pl.ds(pl.multiple_of((it * FF_CHUNKS_PER_ITER + c) * FF_CHUNK, FF_CHUNK),
                         FF_CHUNK)
            f = f + _dot(acts[c], _cols(w_ff2, 0, d_tiles, rows))
        f_buf[...] = f
        for c in range(conv_per_iter):
            conv_chunk(it * conv_per_iter + c)
        return carry

    lax.fori_loop(0, n_iter, middle, 0)

    y_a = _dot(za_buf[...], _cols(w_pool_proj, 0, d_tiles))
    _, _, gate_ffn = split3(mf_b_ref)
    y_ref[...] = y_ref[...] + gate_ffn * _rms(f_buf[...].reshape(bt, tt, D_MODEL), g_post_ffn[...])
    conv = jnp.concatenate(
        [jnp.concatenate([vbuf[(bt + b) * conv_tiles + lt, 0:tt, :]
                          for lt in range(conv_tiles)], axis=1) for b in range(bt)], axis=0)
    mu = jnp.mean(conv, axis=-1, keepdims=True)
    xc = conv - mu
    var = jnp.mean(xc * xc, axis=-1, keepdims=True)
    ln = xc * lax.rsqrt(var + EPS) * ln_g[...] + ln_b[...]
    z_b = (ln * _sigmoid(ln)).astype(_BF16)
    off = D_POOL + 2 * D_CONV
    h = h_buf[...]
    g_a = _sigmoid(_dot(h, w_in[:, off:off + D_MODEL]))
    g_b = _sigmoid(_dot(h, w_in[:, off + D_MODEL:off + 2 * D_MODEL]))
    y_b = _dot(z_b, _cols(w_conv_proj, 0, d_tiles))
    merged_carry[...] = (g_a * y_a + g_b * y_b).astype(_BF16)
    x_carry[...] = x_ref[...]

    @pl.when(jnp.logical_and(t_idx == nt - 1, step < n_tiles))
    def _write_state():
        for b in range(bt):
            for lt in range(pool_tiles):
                npool_ref[b, :, lt * LANE:(lt + 1) * LANE] = (
                    ubuf[b * pool_tiles + lt, POOL_PAD + tt - POOL_HIST:POOL_PAD + tt, :])
            for lt in range(conv_tiles):
                nconv_ref[b, :, lt * LANE:(lt + 1) * LANE] = (
                    vbuf[b * conv_tiles + lt, CONV_PAD + tt - CONV_HIST:CONV_PAD + tt, :])

    ubuf[:, 0:POOL_PAD, :] = ubuf[:, tt:tt + POOL_PAD, :]
    vbuf[0:bt * conv_tiles, 0:CONV_PAD, :] = vbuf[0:bt * conv_tiles, tt:tt + CONV_PAD, :]


def _tile_rows(batch, seq):
    rows = 256
    tt = min(seq, rows)
    bt = max(1, min(batch, rows // tt))
    return bt, tt


def _layer(x, mod_mix, mod_ffn, state, pos0, params):
    batch, seq, _ = x.shape
    bt, tt = _tile_rows(batch, seq)
    n_iter = D_FF // FF_CHUNK // FF_CHUNKS_PER_ITER
    assert batch % bt == 0 and seq % tt == 0 and tt % CONV_ROWS == 0 and tt >= CONV_PAD
    assert (bt * tt // CONV_ROWS) % n_iter == 0
    has_state = state is not None
    m = bt * tt
    nt = seq // tt
    n_tiles = (batch // bt) * nt

    def tile_a(s):
        return jnp.minimum(s, n_tiles - 1)

    def tile_b(s):
        return jnp.maximum(s - 1, 0)

    def per_batch(rows, cols, tile):
        return pl.BlockSpec((bt, rows, cols), lambda s: (tile(s) // nt, 0, 0))

    def per_tile(tile):
        return pl.BlockSpec((bt, tt, D_MODEL), lambda s: (tile(s) // nt, tile(s) % nt, 0))

    def whole(a):
        nd = a.ndim
        return pl.BlockSpec(a.shape, lambda s: (0,) * nd, pipeline_mode=pl.Buffered(1))

    in_specs = [per_tile(tile_a), per_batch(1, 3 * D_MODEL, tile_a),
                per_batch(1, 3 * D_MODEL, tile_b), per_batch(1, 3 * D_MODEL, tile_b)]
    args = [x, mod_mix, mod_mix, mod_ffn]
    if has_state:
        in_specs += [per_batch(POOL_HIST, D_POOL, tile_a), per_batch(CONV_HIST, D_CONV, tile_a)]
        args += list(state)
    in_specs += [whole(p) for p in params]
    args += list(params)

    return pl.pallas_call(
        functools.partial(_layer_body, bt=bt, tt=tt, nt=nt, n_tiles=n_tiles, pos0=pos0,
                          has_state=has_state),
        grid=(n_tiles + 1,),
        in_specs=in_specs,
        out_specs=[per_tile(tile_b), per_batch(POOL_HIST, D_POOL, tile_a),
                   per_batch(CONV_HIST, D_CONV, tile_a)],
        out_shape=[jax.ShapeDtypeStruct(x.shape, _F32),
                   jax.ShapeDtypeStruct((batch, POOL_HIST, D_POOL), _F32),
                   jax.ShapeDtypeStruct((batch, CONV_HIST, D_CONV), _F32)],
        scratch_shapes=[pltpu.VMEM((bt * D_POOL // LANE, POOL_PAD + tt, LANE), _F32),
                        pltpu.VMEM((2 * bt * D_CONV // LANE, CONV_PAD + tt + SUBLANE, LANE),
                                   _F32),
                        pltpu.VMEM((bt, tt, D_MODEL), _F32),
                        pltpu.VMEM((m, D_MODEL), _BF16),
                        pltpu.VMEM((m, D_MODEL), _BF16),
                        pltpu.VMEM((m, D_MODEL), _BF16),
                        pltpu.VMEM((m, D_POOL), _BF16),
                        pltpu.VMEM((m, D_MODEL), _F32),
                        pltpu.VMEM((CONV_WIDTH, SUBLANE, D_CONV), _F32),
                        pltpu.VMEM((D_POOL // LANE, m, LANE), _F32)],
        compiler_params=pltpu.CompilerParams(
            dimension_semantics=("arbitrary",),
            vmem_limit_bytes=VMEM_LIMIT_BYTES),
        name="layer_state" if has_state else "layer",
    )(*args)


def kernel(x_prompt, x_sample, state_pool, state_conv, c_prompt, c_sample, w_ada_mix, b_ada_mix, g_pre_mix, g_post_mix, w_in, w_grp, pool_scale, w_pool_proj, w_dw, b_dw, ln_g, ln_b, w_conv_proj, w_out, w_ada_ffn, b_ada_ffn, g_pre_ffn, g_post_ffn, w_ff1, w_ff2):
    depth = w_in.shape[0]
    assert depth == 1
    l = 0
    bp = x_prompt.shape[0]
    bs = x_sample.shape[0]

    c_all = jnp.concatenate([c_prompt, c_sample], axis=0)
    mod_mix, mod_ffn = _adaln(c_all, w_ada_mix[l], b_ada_mix[l][None, :],
                              w_ada_ffn[l], b_ada_ffn[l][None, :])
    mod_mix = mod_mix[:, None, :]
    mod_ffn = mod_ffn[:, None, :]

    row = lambda a: a[l][None, :]
    params = (row(g_pre_mix), row(g_post_mix), w_in[l].astype(_BF16), w_grp[l].astype(_BF16),
              row(pool_scale), _lane_major(w_pool_proj[l].astype(_BF16)), w_dw[l], row(b_dw),
              row(ln_g), row(ln_b), _lane_major(w_conv_proj[l].astype(_BF16)),
              _lane_major(w_out[l].astype(_BF16)), row(g_pre_ffn), row(g_post_ffn),
              _lane_major(w_ff1[l].astype(_BF16)), _lane_major(w_ff2[l].astype(_BF16)))

    y_p, pool_p, conv_p = _layer(x_prompt, mod_mix[:bp], mod_ffn[:bp], None, 0, params)
    y_s, pool_s, conv_s = _layer(x_sample, mod_mix[bp:bp + bs], mod_ffn[bp:bp + bs],
                                 (state_pool[l], state_conv[l]), PAST_LEN, params)
    return (y_p, y_s, pool_p[None], conv_p[None], pool_s[None], conv_s[None])
```

```python
import functools

import jax
import jax.numpy as jnp
from jax import lax
from jax.experimental import pallas as pl
from jax.experimental.pallas import tpu as pltpu

D_MODEL = 1024
POOL_WINDOWS = (2, 4, 8, 16)
POOL_GROUP = 128
D_POOL = POOL_GROUP * len(POOL_WINDOWS)
POOL_HIST = max(POOL_WINDOWS) - 1
D_CONV = 512
CONV_WIDTH = 31
CONV_HIST = CONV_WIDTH - 1
D_IN = D_POOL + 2 * D_CONV + 2 * D_MODEL
D_FF = 4 * D_MODEL
EPS = 1e-6
PAST_LEN = 1024

LANE = 128
SUBLANE = 8
POOL_PAD = 16
CONV_PAD = 32
ROW_STRIDE = 4
CONV_ROWS = ROW_STRIDE * SUBLANE
FF_CHUNK = 1024
FF_CHUNKS_PER_ITER = 2
ADA_TILE = 512
TILE_ROWS = 512
VMEM_LIMIT_BYTES = 60 * 1024 * 1024

_F32 = jnp.float32
_BF16 = jnp.bfloat16


def _dot(a, b):
    return jnp.dot(a, b, preferred_element_type=_F32)


def _sigmoid(x):
    return 1.0 / (1.0 + jnp.exp(-x))


def _rms(x, g):
    return x * lax.rsqrt(jnp.mean(x * x, axis=-1, keepdims=True) + EPS) * g


def _adaln_body(c_ref, wm_ref, bm_ref, wf_ref, bf_ref, om_ref, of_ref):
    c = c_ref[...]
    s = (c * _sigmoid(c)).astype(_BF16)
    om_ref[...] = _dot(s, wm_ref[...].astype(_BF16)) + bm_ref[...]
    of_ref[...] = _dot(s, wf_ref[...].astype(_BF16)) + bf_ref[...]


def _adaln(c, w_mix, b_mix, w_ffn, b_ffn):
    n = c.shape[0]
    n_out = w_mix.shape[1]
    w_spec = pl.BlockSpec((D_MODEL, ADA_TILE), lambda j: (0, j))
    b_spec = pl.BlockSpec((1, ADA_TILE), lambda j: (0, j))
    o_spec = pl.BlockSpec((n, ADA_TILE), lambda j: (0, j))
    return pl.pallas_call(
        _adaln_body,
        grid=(n_out // ADA_TILE,),
        in_specs=[pl.BlockSpec((n, D_MODEL), lambda j: (0, 0)), w_spec, b_spec, w_spec, b_spec],
        out_specs=[o_spec, o_spec],
        out_shape=[jax.ShapeDtypeStruct((n, n_out), _F32)] * 2,
        name="adaln",
    )(c, w_mix, b_mix, w_ffn, b_ffn)


def _lane_major(w):
    k, n = w.shape
    return w.reshape(k, n // LANE, LANE).transpose(1, 0, 2)


def _cols(w_ref, first_tile, n_tiles, rows=None):
    if rows is None:
        parts = [w_ref[first_tile + q] for q in range(n_tiles)]
    else:
        parts = [w_ref[first_tile + q, rows, :] for q in range(n_tiles)]
    return jnp.concatenate(parts, axis=1)


def _layer_body(*refs, bt, tt, nt, n_tiles, pos0, has_state):
    refs = list(refs)
    x_ref, mm_a_ref, mm_b_ref, mf_b_ref = refs[:4]
    refs = refs[4:]
    if has_state:
        sp_ref, sc_ref = refs[:2]
        refs = refs[2:]
    (g_pre_mix, g_post_mix, w_in, w_grp, pool_scale, w_pool_proj, w_dw, b_dw, ln_g, ln_b,
     w_conv_proj, w_out, g_pre_ffn, g_post_ffn, w_ff1, w_ff2,
     y_ref, npool_ref, nconv_ref,
     ubuf, vbuf, x_carry, merged_carry, h_buf, h2_buf, za_buf, f_buf, wdw_b, d_buf) = refs

    m = bt * tt
    step = pl.program_id(0)
    tile_a = jnp.minimum(step, n_tiles - 1)
    t_idx = tile_a % nt
    d_tiles = D_MODEL // LANE
    pool_tiles = D_POOL // LANE
    conv_tiles = D_CONV // LANE

    @pl.when(step == 0)
    def _init_carry():
        x_carry[...] = jnp.zeros((bt, tt, D_MODEL), _F32)
        merged_carry[...] = jnp.zeros((m, D_MODEL), _BF16)
        for j in range(CONV_WIDTH):
            wdw_b[j] = jnp.broadcast_to(w_dw[j:j + 1, :], (SUBLANE, D_CONV))

    @pl.when(t_idx == 0)
    def _init_history():
        ubuf[:, 0:POOL_PAD, :] = jnp.zeros((bt * pool_tiles, POOL_PAD, LANE), _F32)
        vbuf[0:bt * conv_tiles, 0:CONV_PAD, :] = jnp.zeros((bt * conv_tiles, CONV_PAD, LANE), _F32)
        if has_state:
            for b in range(bt):
                for lt in range(pool_tiles):
                    ubuf[b * pool_tiles + lt, POOL_PAD - POOL_HIST:POOL_PAD, :] = (
                        sp_ref[b, :, lt * LANE:(lt + 1) * LANE])
                for lt in range(conv_tiles):
                    vbuf[b * conv_tiles + lt, CONV_PAD - CONV_HIST:CONV_PAD, :] = (
                        sc_ref[b, :, lt * LANE:(lt + 1) * LANE])

    def split3(mod_ref):
        mod = mod_ref[...]
        return (mod[:, :, 0:D_MODEL], mod[:, :, D_MODEL:2 * D_MODEL],
                mod[:, :, 2 * D_MODEL:3 * D_MODEL])

    out = _dot(merged_carry[...], _cols(w_out, 0, d_tiles)).reshape(bt, tt, D_MODEL)
    x = x_ref[...]
    shift, scale, _ = split3(mm_a_ref)
    h = (_rms(x, g_pre_mix[...]) * (1.0 + scale) + shift).astype(_BF16).reshape(m, D_MODEL)
    h_buf[...] = h

    u = _dot(h, w_in[:, 0:D_POOL])
    for b in range(bt):
        for lt in range(pool_tiles):
            ubuf[b * pool_tiles + lt, POOL_PAD:POOL_PAD + tt, :] = (
                u[b * tt:(b + 1) * tt, lt * LANE:(lt + 1) * LANE])
    v = _dot(h, w_in[:, D_POOL:D_POOL + D_CONV]) * _sigmoid(
        _dot(h, w_in[:, D_POOL + D_CONV:D_POOL + 2 * D_CONV]))
    for b in range(bt):
        for lt in range(conv_tiles):
            vbuf[b * conv_tiles + lt, CONV_PAD:CONV_PAD + tt, :] = (
                v[b * tt:(b + 1) * tt, lt * LANE:(lt + 1) * LANE])
    _, _, gate_mix = split3(mm_b_ref)
    x1 = x_carry[...] + gate_mix * _rms(out, g_post_mix[...])
    y_ref[...] = x1
    shift, scale, _ = split3(mf_b_ref)
    h2_buf[...] = (_rms(x1, g_pre_ffn[...]) * (1.0 + scale) + shift).astype(_BF16).reshape(
        m, D_MODEL)

    pos_base = pos0 + t_idx * tt
    row_iota = ROW_STRIDE * lax.broadcasted_iota(jnp.int32, (SUBLANE, LANE), 0)
    for g, k in enumerate(POOL_WINDOWS):
        for b in range(bt):
            slab = b * pool_tiles + g
            for r0 in range(0, tt, CONV_ROWS):
                for rho in range(ROW_STRIDE):
                    first = POOL_PAD + r0 + rho
                    cur = ubuf[slab, pl.ds(first, SUBLANE, stride=ROW_STRIDE), :]
                    s = cur
                    for j in range(1, k):
                        s = s + ubuf[slab, pl.ds(first - j, SUBLANE, stride=ROW_STRIDE), :]
                    cnt = jnp.minimum(k, pos_base + (r0 + rho + 1) + row_iota).astype(_F32)
                    d_buf[g, pl.ds(b * tt + r0 + rho, SUBLANE, stride=ROW_STRIDE), :] = (
                        s / cnt - cur)
        lo, hi = g * POOL_GROUP, (g + 1) * POOL_GROUP
        za_buf[:, lo:hi] = (_dot(d_buf[g].astype(_BF16), w_grp[g])
                            * pool_scale[:, lo:hi]).astype(_BF16)

    n_ff = D_FF // FF_CHUNK
    n_iter = n_ff // FF_CHUNKS_PER_ITER
    ff_tiles = FF_CHUNK // LANE
    chunks_per_b = tt // CONV_ROWS
    conv_per_iter = bt * chunks_per_b // n_iter
    conv_base = CONV_PAD - CONV_HIST

    def conv_chunk(q):
        b = q // chunks_per_b
        r0 = (q % chunks_per_b) * CONV_ROWS
        for lt in range(conv_tiles):
            lanes = slice(lt * LANE, (lt + 1) * LANE)
            slab = b * conv_tiles + lt
            accs = [None] * ROW_STRIDE
            for o in range(CONV_WIDTH + ROW_STRIDE - 1):
                d_o = vbuf[slab, pl.ds(r0 + (conv_base + o), SUBLANE, stride=ROW_STRIDE), :]
                for rho in range(ROW_STRIDE):
                    j = o - rho
                    if 0 <= j < CONV_WIDTH:
                        term = wdw_b[j, :, lanes] * d_o
                        accs[rho] = term if accs[rho] is None else accs[rho] + term
            for rho in range(ROW_STRIDE):
                vbuf[bt * conv_tiles + slab, pl.ds(r0 + rho, SUBLANE, stride=ROW_STRIDE), :] = (
                    accs[rho] + b_dw[:, lanes])

    f_buf[...] = jnp.zeros((m, D_MODEL), _F32)

    def middle(it, carry):
        h2 = h2_buf[...]
        acts = []
        for c in range(FF_CHUNKS_PER_ITER):
            first = (it * FF_CHUNKS_PER_ITER + c) * ff_tiles
            a = jnp.maximum(_dot(h2, _cols(w_ff1, first, ff_tiles)), 0.0)
            acts.append((a * a).astype(_BF16))
        f = f_buf[...]
        for c in range(FF_CHUNKS_PER_ITER):
            rows = pl.ds(pl.multiple_of((it * FF_CHUNKS_PER_ITER + c) * FF_CHUNK, FF_CHUNK),
                         FF_CHUNK)
            f = f + _dot(acts[c], _cols(w_ff2, 0, d_tiles, rows))
        f_buf[...] = f
        for c in range(conv_per_iter):
            conv_chunk(it * conv_per_iter + c)
        return carry

    lax.fori_loop(0, n_iter, middle, 0)

    y_a = _dot(za_buf[...], _cols(w_pool_proj, 0, d_tiles))
    _, _, gate_ffn = split3(mf_b_ref)
    y_ref[...] = y_ref[...] + gate_ffn * _rms(f_buf[...].reshape(bt, tt, D_MODEL), g_post_ffn[...])
    conv = jnp.concatenate(
        [jnp.concatenate([vbuf[(bt + b) * conv_tiles + lt, 0:tt, :]
                          for lt in range(conv_tiles)], axis=1) for b in range(bt)], axis=0)
    mu = jnp.mean(conv, axis=-1, keepdims=True)
    xc = conv - mu
    var = jnp.mean(xc * xc, axis=-1, keepdims=True)
    ln = xc * lax.rsqrt(var + EPS) * ln_g[...] + ln_b[...]
    z_b = (ln * _sigmoid(ln)).astype(_BF16)
    off = D_POOL + 2 * D_CONV
    h = h_buf[...]
    g_a = _sigmoid(_dot(h, w_in[:, off:off + D_MODEL]))
    g_b = _sigmoid(_dot(h, w_in[:, off + D_MODEL:off + 2 * D_MODEL]))
    y_b = _dot(z_b, _cols(w_conv_proj, 0, d_tiles))
    merged_carry[...] = (g_a * y_a + g_b * y_b).astype(_BF16)
    x_carry[...] = x_ref[...]

    @pl.when(jnp.logical_and(t_idx == nt - 1, step < n_tiles))
    def _write_state():
        for b in range(bt):
            for lt in range(pool_tiles):
                npool_ref[b, :, lt * LANE:(lt + 1) * LANE] = (
                    ubuf[b * pool_tiles + lt, POOL_PAD + tt - POOL_HIST:POOL_PAD + tt, :])
            for lt in range(conv_tiles):
                nconv_ref[b, :, lt * LANE:(lt + 1) * LANE] = (
                    vbuf[b * conv_tiles + lt, CONV_PAD + tt - CONV_HIST:CONV_PAD + tt, :])

    ubuf[:, 0:POOL_PAD, :] = ubuf[:, tt:tt + POOL_PAD, :]
    vbuf[0:bt * conv_tiles, 0:CONV_PAD, :] = vbuf[0:bt * conv_tiles, tt:tt + CONV_PAD, :]


def _tile_rows(batch, seq):
    rows = TILE_ROWS
    tt = min(seq, rows)
    bt = max(1, min(batch, rows // tt))
    return bt, tt


def _layer(x, mod_mix, mod_ffn, state, pos0, params):
    batch, seq, _ = x.shape
    bt, tt = _tile_rows(batch, seq)
    n_iter = D_FF // FF_CHUNK // FF_CHUNKS_PER_ITER
    assert batch % bt == 0 and seq % tt == 0 and tt % CONV_ROWS == 0 and tt >= CONV_PAD
    assert (bt * tt // CONV_ROWS) % n_iter == 0
    has_state = state is not None
    m = bt * tt
    nt = seq // tt
    n_tiles = (batch // bt) * nt

    def tile_a(s):
        return jnp.minimum(s, n_tiles - 1)

    def tile_b(s):
        return jnp.maximum(s - 1, 0)

    def per_batch(rows, cols, tile):
        return pl.BlockSpec((bt, rows, cols), lambda s: (tile(s) // nt, 0, 0))

    def per_tile(tile):
        return pl.BlockSpec((bt, tt, D_MODEL), lambda s: (tile(s) // nt, tile(s) % nt, 0))

    def whole(a):
        nd = a.ndim
        return pl.BlockSpec(a.shape, lambda s: (0,) * nd, pipeline_mode=pl.Buffered(1))

    in_specs = [per_tile(tile_a), per_batch(1, 3 * D_MODEL, tile_a),
                per_batch(1, 3 * D_MODEL, tile_b), per_batch(1, 3 * D_MODEL, tile_b)]
    args = [x, mod_mix, mod_mix, mod_ffn]
    if has_state:
        in_specs += [per_batch(POOL_HIST, D_POOL, tile_a), per_batch(CONV_HIST, D_CONV, tile_a)]
        args += list(state)
    in_specs += [whole(p) for p in params]
    args += list(params)

    return pl.pallas_call(
        functools.partial(_layer_body, bt=bt, tt=tt, nt=nt, n_tiles=n_tiles, pos0=pos0,
                          has_state=has_state),
        grid=(n_tiles + 1,),
        in_specs=in_specs,
        out_specs=[per_tile(tile_b), per_batch(POOL_HIST, D_POOL, tile_a),
                   per_batch(CONV_HIST, D_CONV, tile_a)],
        out_shape=[jax.ShapeDtypeStruct(x.shape, _F32),
                   jax.ShapeDtypeStruct((batch, POOL_HIST, D_POOL), _F32),
                   jax.ShapeDtypeStruct((batch, CONV_HIST, D_CONV), _F32)],
        scratch_shapes=[pltpu.VMEM((bt * D_POOL // LANE, POOL_PAD + tt, LANE), _F32),
                        pltpu.VMEM((2 * bt * D_CONV // LANE, CONV_PAD + tt, LANE), _F32),
                        pltpu.VMEM((bt, tt, D_MODEL), _F32),
                        pltpu.VMEM((m, D_MODEL), _BF16),
                        pltpu.VMEM((m, D_MODEL), _BF16),
                        pltpu.VMEM((m, D_MODEL), _BF16),
                        pltpu.VMEM((m, D_POOL), _BF16),
                        pltpu.VMEM((m, D_MODEL), _F32),
                        pltpu.VMEM((CONV_WIDTH, SUBLANE, D_CONV), _F32),
                        pltpu.VMEM((D_POOL // LANE, m, LANE), _F32)],
        compiler_params=pltpu.CompilerParams(
            dimension_semantics=("arbitrary",),
            vmem_limit_bytes=VMEM_LIMIT_BYTES),
        name="layer_state" if has_state else "layer",
    )(*args)


def kernel(x_prompt, x_sample, state_pool, state_conv, c_prompt, c_sample, w_ada_mix, b_ada_mix, g_pre_mix, g_post_mix, w_in, w_grp, pool_scale, w_pool_proj, w_dw, b_dw, ln_g, ln_b, w_conv_proj, w_out, w_ada_ffn, b_ada_ffn, g_pre_ffn, g_post_ffn, w_ff1, w_ff2):
    depth = w_in.shape[0]
    assert depth == 1
    l = 0
    bp = x_prompt.shape[0]
    bs = x_sample.shape[0]

    c_all = jnp.concatenate([c_prompt, c_sample], axis=0)
    mod_mix, mod_ffn = _adaln(c_all, w_ada_mix[l], b_ada_mix[l][None, :],
                              w_ada_ffn[l], b_ada_ffn[l][None, :])
    mod_mix = mod_mix[:, None, :]
    mod_ffn = mod_ffn[:, None, :]

    row = lambda a: a[l][None, :]
    params = (row(g_pre_mix), row(g_post_mix), w_in[l].astype(_BF16), w_grp[l].astype(_BF16),
              row(pool_scale), _lane_major(w_pool_proj[l].astype(_BF16)), w_dw[l], row(b_dw),
              row(ln_g), row(ln_b), _lane_major(w_conv_proj[l].astype(_BF16)),
              _lane_major(w_out[l].astype(_BF16)), row(g_pre_ffn), row(g_post_ffn),
              _lane_major(w_ff1[l].astype(_BF16)), _lane_major(w_ff2[l].astype(_BF16)))

    y_p, pool_p, conv_p = _layer(x_prompt, mod_mix[:bp], mod_ffn[:bp], None, 0, params)
    y_s, pool_s, conv_s = _layer(x_sample, mod_mix[bp:bp + bs], mod_ffn[bp:bp + bs],
                                 (state_pool[l], state_conv[l]), PAST_LEN, params)
    return (y_p, y_s, pool_p[None], conv_p[None], pool_s[None], conv_s[None])
```

```python
import functools

import jax
import jax.numpy as jnp
from jax import lax
from jax.experimental import pallas as pl
from jax.experimental.pallas import tpu as pltpu

D_MODEL = 1024
POOL_WINDOWS = (2, 4, 8, 16)
POOL_GROUP = 128
D_POOL = POOL_GROUP * len(POOL_WINDOWS)
POOL_HIST = max(POOL_WINDOWS) - 1
D_CONV = 512
CONV_WIDTH = 31
CONV_HIST = CONV_WIDTH - 1
D_IN = D_POOL + 2 * D_CONV + 2 * D_MODEL
D_FF = 4 * D_MODEL
EPS = 1e-6
PAST_LEN = 1024

LANE = 128
SUBLANE = 8
POOL_PAD = 16
CONV_PAD = 32
ROW_STRIDE = 4
CONV_ROWS = ROW_STRIDE * SUBLANE
FF_CHUNK = 1024
FF_CHUNKS_PER_ITER = 2
ADA_TILE = 512
TILE_ROWS = 512
MIN_TILES_FULL = 16
VMEM_LIMIT_BYTES = 60 * 1024 * 1024

_F32 = jnp.float32
_BF16 = jnp.bfloat16


def _dot(a, b):
    return jnp.dot(a, b, preferred_element_type=_F32)


NEG_LOG2_E = -1.4426950408889634


def _sigmoid(x):
    return 1.0 / (1.0 + jnp.exp2(x * NEG_LOG2_E))


def _unit_rms(x):
    return x * lax.rsqrt(jnp.mean(x * x, axis=-1, keepdims=True) + EPS)


def _adaln_body(c_ref, wm_ref, bm_ref, wf_ref, bf_ref, om_ref, of_ref):
    c = c_ref[...]
    s = (c * _sigmoid(c)).astype(_BF16)
    om_ref[...] = _dot(s, wm_ref[...].astype(_BF16)) + bm_ref[...]
    of_ref[...] = _dot(s, wf_ref[...].astype(_BF16)) + bf_ref[...]


def _adaln(c, w_mix, b_mix, w_ffn, b_ffn):
    n = c.shape[0]
    n_out = w_mix.shape[1]
    w_spec = pl.BlockSpec((D_MODEL, ADA_TILE), lambda j: (0, j))
    b_spec = pl.BlockSpec((1, ADA_TILE), lambda j: (0, j))
    o_spec = pl.BlockSpec((n, ADA_TILE), lambda j: (0, j))
    return pl.pallas_call(
        _adaln_body,
        grid=(n_out // ADA_TILE,),
        in_specs=[pl.BlockSpec((n, D_MODEL), lambda j: (0, 0)), w_spec, b_spec, w_spec, b_spec],
        out_specs=[o_spec, o_spec],
        out_shape=[jax.ShapeDtypeStruct((n, n_out), _F32)] * 2,
        name="adaln",
    )(c, w_mix, b_mix, w_ffn, b_ffn)


def _lane_major(w):
    k, n = w.shape
    return w.reshape(k, n // LANE, LANE).transpose(1, 0, 2)


def _cols(w_ref, first_tile, n_tiles, rows=None):
    if rows is None:
        parts = [w_ref[first_tile + q] for q in range(n_tiles)]
    else:
        parts = [w_ref[first_tile + q, rows, :] for q in range(n_tiles)]
    return jnp.concatenate(parts, axis=1)


def _layer_body(*refs, bt, tt, nt, n_tiles, pos0, has_state):
    refs = list(refs)
    x_ref, mm_a_ref, mm_b_ref, mf_b_ref = refs[:4]
    refs = refs[4:]
    if has_state:
        sp_ref, sc_ref = refs[:2]
        refs = refs[2:]
    (g_pre_mix, g_post_mix, w_in, w_grp, pool_scale, w_pool_proj, w_dw, b_dw, ln_g, ln_b,
     w_conv_proj, w_out, g_pre_ffn, g_post_ffn, w_ff1, w_ff2,
     y_ref, npool_ref, nconv_ref,
     ubuf, vbuf, x_carry, merged_carry, h_buf, h2_buf, za_buf, f_buf, wdw_b, d_buf) = refs

    m = bt * tt
    step = pl.program_id(0)
    tile_a = jnp.minimum(step, n_tiles - 1)
    t_idx = tile_a % nt
    d_tiles = D_MODEL // LANE
    pool_tiles = D_POOL // LANE
    conv_tiles = D_CONV // LANE

    @pl.when(step == 0)
    def _init_carry():
        x_carry[...] = jnp.zeros((bt, tt, D_MODEL), _F32)
        merged_carry[...] = jnp.zeros((m, D_MODEL), _BF16)
        for j in range(CONV_WIDTH):
            wdw_b[j] = jnp.broadcast_to(w_dw[j:j + 1, :], (SUBLANE, D_CONV))

    @pl.when(t_idx == 0)
    def _init_history():
        ubuf[:, 0:POOL_PAD, :] = jnp.zeros((bt * pool_tiles, POOL_PAD, LANE), _F32)
        vbuf[0:bt * conv_tiles, 0:CONV_PAD, :] = jnp.zeros((bt * conv_tiles, CONV_PAD, LANE), _F32)
        if has_state:
            for b in range(bt):
                for lt in range(pool_tiles):
                    ubuf[b * pool_tiles + lt, POOL_PAD - POOL_HIST:POOL_PAD, :] = (
                        sp_ref[b, :, lt * LANE:(lt + 1) * LANE])
                for lt in range(conv_tiles):
                    vbuf[b * conv_tiles + lt, CONV_PAD - CONV_HIST:CONV_PAD, :] = (
                        sc_ref[b, :, lt * LANE:(lt + 1) * LANE])

    def split3(mod_ref):
        mod = mod_ref[...]
        return (mod[:, :, 0:D_MODEL], mod[:, :, D_MODEL:2 * D_MODEL],
                mod[:, :, 2 * D_MODEL:3 * D_MODEL])

    out = _dot(merged_carry[...], _cols(w_out, 0, d_tiles)).reshape(bt, tt, D_MODEL)
    x = x_ref[...]
    shift, scale, _ = split3(mm_a_ref)
    h = (_unit_rms(x) * (g_pre_mix[...] * (1.0 + scale)) + shift).astype(_BF16).reshape(m, D_MODEL)
    h_buf[...] = h

    u = _dot(h, w_in[:, 0:D_POOL])
    for b in range(bt):
        for lt in range(pool_tiles):
            ubuf[b * pool_tiles + lt, POOL_PAD:POOL_PAD + tt, :] = (
                u[b * tt:(b + 1) * tt, lt * LANE:(lt + 1) * LANE])
    v = _dot(h, w_in[:, D_POOL:D_POOL + D_CONV]) * _sigmoid(
        _dot(h, w_in[:, D_POOL + D_CONV:D_POOL + 2 * D_CONV]))
    for b in range(bt):
        for lt in range(conv_tiles):
            vbuf[b * conv_tiles + lt, CONV_PAD:CONV_PAD + tt, :] = (
                v[b * tt:(b + 1) * tt, lt * LANE:(lt + 1) * LANE])
    _, _, gate_mix = split3(mm_b_ref)
    x1 = x_carry[...] + _unit_rms(out) * (gate_mix * g_post_mix[...])
    y_ref[...] = x1
    shift, scale, _ = split3(mf_b_ref)
    h2_buf[...] = (_unit_rms(x1) * (g_pre_ffn[...] * (1.0 + scale)) + shift).astype(
        _BF16).reshape(m, D_MODEL)

    pos_base = pos0 + t_idx * tt
    row_iota = ROW_STRIDE * lax.broadcasted_iota(jnp.int32, (SUBLANE, LANE), 0)
    for g, k in enumerate(POOL_WINDOWS):
        for b in range(bt):
            slab = b * pool_tiles + g
            for r0 in range(0, tt, CONV_ROWS):
                for rho in range(ROW_STRIDE):
                    first = POOL_PAD + r0 + rho
                    cur = ubuf[slab, pl.ds(first, SUBLANE, stride=ROW_STRIDE), :]
                    s = cur
                    for j in range(1, k):
                        s = s + ubuf[slab, pl.ds(first - j, SUBLANE, stride=ROW_STRIDE), :]
                    cnt = jnp.minimum(k, pos_base + (r0 + rho + 1) + row_iota).astype(_F32)
                    d_buf[g, pl.ds(b * tt + r0 + rho, SUBLANE, stride=ROW_STRIDE), :] = (
                        s / cnt - cur)
        lo, hi = g * POOL_GROUP, (g + 1) * POOL_GROUP
        za_buf[:, lo:hi] = (_dot(d_buf[g].astype(_BF16), w_grp[g])
                            * pool_scale[:, lo:hi]).astype(_BF16)

    n_ff = D_FF // FF_CHUNK
    n_iter = n_ff // FF_CHUNKS_PER_ITER
    ff_tiles = FF_CHUNK // LANE
    chunks_per_b = tt // CONV_ROWS
    conv_per_iter = bt * chunks_per_b // n_iter
    conv_base = CONV_PAD - CONV_HIST

    def conv_chunk(q):
        b = q // chunks_per_b
        r0 = (q % chunks_per_b) * CONV_ROWS
        for lt in range(conv_tiles):
            lanes = slice(lt * LANE, (lt + 1) * LANE)
            slab = b * conv_tiles + lt
            accs = [None] * ROW_STRIDE
            for o in range(CONV_WIDTH + ROW_STRIDE - 1):
                d_o = vbuf[slab, pl.ds(r0 + (conv_base + o), SUBLANE, stride=ROW_STRIDE), :]
                for rho in range(ROW_STRIDE):
                    j = o - rho
                    if 0 <= j < CONV_WIDTH:
                        term = wdw_b[j, :, lanes] * d_o
                        accs[rho] = term if accs[rho] is None else accs[rho] + term
            for rho in range(ROW_STRIDE):
                vbuf[bt * conv_tiles + slab, pl.ds(r0 + rho, SUBLANE, stride=ROW_STRIDE), :] = (
                    accs[rho] + b_dw[:, lanes])

    f_buf[...] = jnp.zeros((m, D_MODEL), _F32)

    def middle(it, carry):
        h2 = h2_buf[...]
        acts = []
        for c in range(FF_CHUNKS_PER_ITER):
            first = (it * FF_CHUNKS_PER_ITER + c) * ff_tiles
            a = jnp.maximum(_dot(h2, _cols(w_ff1, first, ff_tiles)), 0.0)
            acts.append((a * a).astype(_BF16))
        f = f_buf[...]
        for c in range(FF_CHUNKS_PER_ITER):
            rows = pl.ds(pl.multiple_of((it * FF_CHUNKS_PER_ITER + c) * FF_CHUNK, FF_CHUNK),
                         FF_CHUNK)
            f = f + _dot(acts[c], _cols(w_ff2, 0, d_tiles, rows))
        f_buf[...] = f
        for c in range(conv_per_iter):
            conv_chunk(it * conv_per_iter + c)
        return carry

    lax.fori_loop(0, n_iter, middle, 0)

    y_a = _dot(za_buf[...], _cols(w_pool_proj, 0, d_tiles))
    _, _, gate_ffn = split3(mf_b_ref)
    y_ref[...] = y_ref[...] + _unit_rms(f_buf[...].reshape(bt, tt, D_MODEL)) * (
        gate_ffn * g_post_ffn[...])
    conv = jnp.concatenate(
        [jnp.concatenate([vbuf[(bt + b) * conv_tiles + lt, 0:tt, :]
                          for lt in range(conv_tiles)], axis=1) for b in range(bt)], axis=0)
    mu = jnp.mean(conv, axis=-1, keepdims=True)
    xc = conv - mu
    var = jnp.mean(xc * xc, axis=-1, keepdims=True)
    ln = xc * lax.rsqrt(var + EPS) * ln_g[...] + ln_b[...]
    z_b = (ln * _sigmoid(ln)).astype(_BF16)
    off = D_POOL + 2 * D_CONV
    h = h_buf[...]
    g_a = _sigmoid(_dot(h, w_in[:, off:off + D_MODEL]))
    g_b = _sigmoid(_dot(h, w_in[:, off + D_MODEL:off + 2 * D_MODEL]))
    y_b = _dot(z_b, _cols(w_conv_proj, 0, d_tiles))
    merged_carry[...] = (g_a * y_a + g_b * y_b).astype(_BF16)
    x_carry[...] = x_ref[...]

    @pl.when(jnp.logical_and(t_idx == nt - 1, step < n_tiles))
    def _write_state():
        for b in range(bt):
            for lt in range(pool_tiles):
                npool_ref[b, :, lt * LANE:(lt + 1) * LANE] = (
                    ubuf[b * pool_tiles + lt, POOL_PAD + tt - POOL_HIST:POOL_PAD + tt, :])
            for lt in range(conv_tiles):
                nconv_ref[b, :, lt * LANE:(lt + 1) * LANE] = (
                    vbuf[b * conv_tiles + lt, CONV_PAD + tt - CONV_HIST:CONV_PAD + tt, :])

    ubuf[:, 0:POOL_PAD, :] = ubuf[:, tt:tt + POOL_PAD, :]
    vbuf[0:bt * conv_tiles, 0:CONV_PAD, :] = vbuf[0:bt * conv_tiles, tt:tt + CONV_PAD, :]


def _tile_rows(batch, seq):
    rows = TILE_ROWS if batch * seq >= MIN_TILES_FULL * TILE_ROWS else TILE_ROWS // 2
    tt = min(seq, rows)
    bt = max(1, min(batch, rows // tt))
    return bt, tt


def _layer(x, mod_mix, mod_ffn, state, pos0, params):
    batch, seq, _ = x.shape
    bt, tt = _tile_rows(batch, seq)
    n_iter = D_FF // FF_CHUNK // FF_CHUNKS_PER_ITER
    assert batch % bt == 0 and seq % tt == 0 and tt % CONV_ROWS == 0 and tt >= CONV_PAD
    assert (bt * tt // CONV_ROWS) % n_iter == 0
    has_state = state is not None
    m = bt * tt
    nt = seq // tt
    n_tiles = (batch // bt) * nt

    def tile_a(s):
        return jnp.minimum(s, n_tiles - 1)

    def tile_b(s):
        return jnp.maximum(s - 1, 0)

    def per_batch(rows, cols, tile):
        return pl.BlockSpec((bt, rows, cols), lambda s: (tile(s) // nt, 0, 0))

    def per_tile(tile):
        return pl.BlockSpec((bt, tt, D_MODEL), lambda s: (tile(s) // nt, tile(s) % nt, 0))

    def whole(a):
        nd = a.ndim
        return pl.BlockSpec(a.shape, lambda s: (0,) * nd, pipeline_mode=pl.Buffered(1))

    in_specs = [per_tile(tile_a), per_batch(1, 3 * D_MODEL, tile_a),
                per_batch(1, 3 * D_MODEL, tile_b), per_batch(1, 3 * D_MODEL, tile_b)]
    args = [x, mod_mix, mod_mix, mod_ffn]
    if has_state:
        in_specs += [per_batch(POOL_HIST, D_POOL, tile_a), per_batch(CONV_HIST, D_CONV, tile_a)]
        args += list(state)
    in_specs += [whole(p) for p in params]
    args += list(params)

    return pl.pallas_call(
        functools.partial(_layer_body, bt=bt, tt=tt, nt=nt, n_tiles=n_tiles, pos0=pos0,
                          has_state=has_state),
        grid=(n_tiles + 1,),
        in_specs=in_specs,
        out_specs=[per_tile(tile_b), per_batch(POOL_HIST, D_POOL, tile_a),
                   per_batch(CONV_HIST, D_CONV, tile_a)],
        out_shape=[jax.ShapeDtypeStruct(x.shape, _F32),
                   jax.ShapeDtypeStruct((batch, POOL_HIST, D_POOL), _F32),
                   jax.ShapeDtypeStruct((batch, CONV_HIST, D_CONV), _F32)],
        scratch_shapes=[pltpu.VMEM((bt * D_POOL // LANE, POOL_PAD + tt, LANE), _F32),
                        pltpu.VMEM((2 * bt * D_CONV // LANE, CONV_PAD + tt, LANE), _F32),
                        pltpu.VMEM((bt, tt, D_MODEL), _F32),
                        pltpu.VMEM((m, D_MODEL), _BF16),
                        pltpu.VMEM((m, D_MODEL), _BF16),
                        pltpu.VMEM((m, D_MODEL), _BF16),
                        pltpu.VMEM((m, D_POOL), _BF16),
                        pltpu.VMEM((m, D_MODEL), _F32),
                        pltpu.VMEM((CONV_WIDTH, SUBLANE, D_CONV), _F32),
                        pltpu.VMEM((D_POOL // LANE, m, LANE), _F32)],
        compiler_params=pltpu.CompilerParams(
            dimension_semantics=("arbitrary",),
            vmem_limit_bytes=VMEM_LIMIT_BYTES),
        name="layer_state" if has_state else "layer",
    )(*args)


def kernel(x_prompt, x_sample, state_pool, state_conv, c_prompt, c_sample, w_ada_mix, b_ada_mix, g_pre_mix, g_post_mix, w_in, w_grp, pool_scale, w_pool_proj, w_dw, b_dw, ln_g, ln_b, w_conv_proj, w_out, w_ada_ffn, b_ada_ffn, g_pre_ffn, g_post_ffn, w_ff1, w_ff2):
    depth = w_in.shape[0]
    assert depth == 1
    l = 0
    bp = x_prompt.shape[0]
    bs = x_sample.shape[0]

    c_all = jnp.concatenate([c_prompt, c_sample], axis=0)
    mod_mix, mod_ffn = _adaln(c_all, w_ada_mix[l], b_ada_mix[l][None, :],
                              w_ada_ffn[l], b_ada_ffn[l][None, :])
    mod_mix = mod_mix[:, None, :]
    mod_ffn = mod_ffn[:, None, :]

    row = lambda a: a[l][None, :]
    params = (row(g_pre_mix), row(g_post_mix), w_in[l].astype(_BF16), w_grp[l].astype(_BF16),
              row(pool_scale), _lane_major(w_pool_proj[l].astype(_BF16)), w_dw[l], row(b_dw),
              row(ln_g), row(ln_b), _lane_major(w_conv_proj[l].astype(_BF16)),
              _lane_major(w_out[l].astype(_BF16)), row(g_pre_ffn), row(g_post_ffn),
              _lane_major(w_ff1[l].astype(_BF16)), _lane_major(w_ff2[l].astype(_BF16)))

    y_p, pool_p, conv_p = _layer(x_prompt, mod_mix[:bp], mod_ffn[:bp], None, 0, params)
    y_s, pool_s, conv_s = _layer(x_sample, mod_mix[bp:bp + bs], mod_ffn[bp:bp + bs],
                                 (state_pool[l], state_conv[l]), PAST_LEN, params)
    return (y_p, y_s, pool_p[None], conv_p[None], pool_s[None], conv_s[None])
```

```python
import functools

import jax
import jax.numpy as jnp
from jax import lax
from jax.experimental import pallas as pl
from jax.experimental.pallas import tpu as pltpu

D_MODEL = 1024
POOL_WINDOWS = (2, 4, 8, 16)
POOL_GROUP = 128
D_POOL = POOL_GROUP * len(POOL_WINDOWS)
POOL_HIST = max(POOL_WINDOWS) - 1
D_CONV = 512
CONV_WIDTH = 31
CONV_HIST = CONV_WIDTH - 1
D_IN = D_POOL + 2 * D_CONV + 2 * D_MODEL
D_FF = 4 * D_MODEL
EPS = 1e-6
PAST_LEN = 1024

LANE = 128
COL_PAD = LANE
SUBLANE = 8
POOL_PAD = 16
CONV_PAD = 32
ROW_STRIDE = 4
CONV_ROWS = ROW_STRIDE * SUBLANE
FF_CHUNK = 1024
FF_CHUNKS_PER_ITER = 2
ADA_TILE = 512
TILE_ROWS = 512
MIN_TILES_FULL = 16
VMEM_LIMIT_BYTES = 60 * 1024 * 1024

_F32 = jnp.float32
_BF16 = jnp.bfloat16


def _dot(a, b):
    return jnp.dot(a, b, preferred_element_type=_F32)


NEG_LOG2_E = -1.4426950408889634


def _sigmoid(x):
    return 1.0 / (1.0 + jnp.exp2(x * NEG_LOG2_E))


def _unit_rms(x):
    return x * lax.rsqrt(jnp.mean(x * x, axis=-1, keepdims=True) + EPS)


def _adaln_body(c_ref, wm_ref, bm_ref, wf_ref, bf_ref, om_ref, of_ref):
    c = c_ref[...]
    s = (c * _sigmoid(c)).astype(_BF16)
    om_ref[...] = _dot(s, wm_ref[...].astype(_BF16)) + bm_ref[...]
    of_ref[...] = _dot(s, wf_ref[...].astype(_BF16)) + bf_ref[...]


def _adaln(c, w_mix, b_mix, w_ffn, b_ffn):
    n = c.shape[0]
    n_out = w_mix.shape[1]
    w_spec = pl.BlockSpec((D_MODEL, ADA_TILE), lambda j: (0, j))
    b_spec = pl.BlockSpec((1, ADA_TILE), lambda j: (0, j))
    o_spec = pl.BlockSpec((n, ADA_TILE), lambda j: (0, j))
    return pl.pallas_call(
        _adaln_body,
        grid=(n_out // ADA_TILE,),
        in_specs=[pl.BlockSpec((n, D_MODEL), lambda j: (0, 0)), w_spec, b_spec, w_spec, b_spec],
        out_specs=[o_spec, o_spec],
        out_shape=[jax.ShapeDtypeStruct((n, n_out), _F32)] * 2,
        name="adaln",
    )(c, w_mix, b_mix, w_ffn, b_ffn)


def _pad_cols(w):
    return jnp.pad(w, [(0, 0)] * (w.ndim - 1) + [(0, COL_PAD)])


def _chunked_cols(w, chunk):
    k, n = w.shape
    return w.reshape(k, n // chunk, chunk).transpose(1, 0, 2)


def _layer_body(*refs, bt, tt, nt, n_tiles, pos0, has_state):
    refs = list(refs)
    x_ref, mm_a_ref, mm_b_ref, mf_b_ref = refs[:4]
    refs = refs[4:]
    if has_state:
        sp_ref, sc_ref = refs[:2]
        refs = refs[2:]
    (g_pre_mix, g_post_mix, w_in, w_grp, pool_scale, w_pool_proj, w_dw, b_dw, ln_g, ln_b,
     w_conv_proj, w_out, g_pre_ffn, g_post_ffn, w_ff1, w_ff2,
     y_ref, npool_ref, nconv_ref,
     ubuf, vbuf, x_carry, merged_carry, h_buf, h2_buf, za_buf, f_buf, wdw_b, d_buf) = refs

    m = bt * tt
    step = pl.program_id(0)
    tile_a = jnp.minimum(step, n_tiles - 1)
    t_idx = tile_a % nt
    pool_tiles = D_POOL // LANE
    conv_tiles = D_CONV // LANE

    @pl.when(step == 0)
    def _init_carry():
        x_carry[...] = jnp.zeros((bt, tt, D_MODEL), _F32)
        merged_carry[...] = jnp.zeros((m, D_MODEL), _BF16)
        for j in range(CONV_WIDTH):
            wdw_b[j] = jnp.broadcast_to(w_dw[j:j + 1, :], (SUBLANE, D_CONV))

    @pl.when(t_idx == 0)
    def _init_history():
        ubuf[:, 0:POOL_PAD, :] = jnp.zeros((bt * pool_tiles, POOL_PAD, LANE), _F32)
        vbuf[0:bt * conv_tiles, 0:CONV_PAD, :] = jnp.zeros((bt * conv_tiles, CONV_PAD, LANE), _F32)
        if has_state:
            for b in range(bt):
                for lt in range(pool_tiles):
                    ubuf[b * pool_tiles + lt, POOL_PAD - POOL_HIST:POOL_PAD, :] = (
                        sp_ref[b, :, lt * LANE:(lt + 1) * LANE])
                for lt in range(conv_tiles):
                    vbuf[b * conv_tiles + lt, CONV_PAD - CONV_HIST:CONV_PAD, :] = (
                        sc_ref[b, :, lt * LANE:(lt + 1) * LANE])

    def split3(mod_ref):
        mod = mod_ref[...]
        return (mod[:, :, 0:D_MODEL], mod[:, :, D_MODEL:2 * D_MODEL],
                mod[:, :, 2 * D_MODEL:3 * D_MODEL])

    out = _dot(merged_carry[...], w_out[:, 0:D_MODEL]).reshape(bt, tt, D_MODEL)
    x = x_ref[...]
    shift, scale, _ = split3(mm_a_ref)
    h = (_unit_rms(x) * (g_pre_mix[...] * (1.0 + scale)) + shift).astype(_BF16).reshape(m, D_MODEL)
    h_buf[...] = h

    u = _dot(h, w_in[:, 0:D_POOL])
    for b in range(bt):
        for lt in range(pool_tiles):
            ubuf[b * pool_tiles + lt, POOL_PAD:POOL_PAD + tt, :] = (
                u[b * tt:(b + 1) * tt, lt * LANE:(lt + 1) * LANE])
    v = _dot(h, w_in[:, D_POOL:D_POOL + D_CONV]) * _sigmoid(
        _dot(h, w_in[:, D_POOL + D_CONV:D_POOL + 2 * D_CONV]))
    for b in range(bt):
        for lt in range(conv_tiles):
            vbuf[b * conv_tiles + lt, CONV_PAD:CONV_PAD + tt, :] = (
                v[b * tt:(b + 1) * tt, lt * LANE:(lt + 1) * LANE])
    _, _, gate_mix = split3(mm_b_ref)
    x1 = x_carry[...] + _unit_rms(out) * (gate_mix * g_post_mix[...])
    y_ref[...] = x1
    shift, scale, _ = split3(mf_b_ref)
    h2_buf[...] = (_unit_rms(x1) * (g_pre_ffn[...] * (1.0 + scale)) + shift).astype(
        _BF16).reshape(m, D_MODEL)

    pos_base = pos0 + t_idx * tt
    row_iota = ROW_STRIDE * lax.broadcasted_iota(jnp.int32, (SUBLANE, LANE), 0)
    for g, k in enumerate(POOL_WINDOWS):
        for b in range(bt):
            slab = b * pool_tiles + g
            for r0 in range(0, tt, CONV_ROWS):
                for rho in range(ROW_STRIDE):
                    first = POOL_PAD + r0 + rho
                    cur = ubuf[slab, pl.ds(first, SUBLANE, stride=ROW_STRIDE), :]
                    s = cur
                    for j in range(1, k):
                        s = s + ubuf[slab, pl.ds(first - j, SUBLANE, stride=ROW_STRIDE), :]
                    cnt = jnp.minimum(k, pos_base + (r0 + rho + 1) + row_iota).astype(_F32)
                    d_buf[g, pl.ds(b * tt + r0 + rho, SUBLANE, stride=ROW_STRIDE), :] = (
                        s / cnt - cur)
        lo, hi = g * POOL_GROUP, (g + 1) * POOL_GROUP
        za_buf[:, lo:hi] = (_dot(d_buf[g].astype(_BF16), w_grp[g])
                            * pool_scale[:, lo:hi]).astype(_BF16)

    n_ff = D_FF // FF_CHUNK
    n_iter = n_ff // FF_CHUNKS_PER_ITER
    chunks_per_b = tt // CONV_ROWS
    conv_per_iter = bt * chunks_per_b // n_iter
    conv_base = CONV_PAD - CONV_HIST

    def conv_chunk(q):
        b = q // chunks_per_b
        r0 = (q % chunks_per_b) * CONV_ROWS
        for lt in range(conv_tiles):
            lanes = slice(lt * LANE, (lt + 1) * LANE)
            slab = b * conv_tiles + lt
            accs = [None] * ROW_STRIDE
            for o in range(CONV_WIDTH + ROW_STRIDE - 1):
                d_o = vbuf[slab, pl.ds(r0 + (conv_base + o), SUBLANE, stride=ROW_STRIDE), :]
                for rho in range(ROW_STRIDE):
                    j = o - rho
                    if 0 <= j < CONV_WIDTH:
                        term = wdw_b[j, :, lanes] * d_o
                        accs[rho] = term if accs[rho] is None else accs[rho] + term
            for rho in range(ROW_STRIDE):
                vbuf[bt * conv_tiles + slab, pl.ds(r0 + rho, SUBLANE, stride=ROW_STRIDE), :] = (
                    accs[rho] + b_dw[:, lanes])

    f_buf[...] = jnp.zeros((m, D_MODEL), _F32)

    def middle(it, carry):
        h2 = h2_buf[...]
        acts = []
        for c in range(FF_CHUNKS_PER_ITER):
            a = jnp.maximum(_dot(h2, w_ff1[it * FF_CHUNKS_PER_ITER + c, :, 0:FF_CHUNK]), 0.0)
            acts.append((a * a).astype(_BF16))
        f = f_buf[...]
        for c in range(FF_CHUNKS_PER_ITER):
            f = f + _dot(acts[c], w_ff2[it * FF_CHUNKS_PER_ITER + c, :, 0:D_MODEL])
        f_buf[...] = f
        for c in range(conv_per_iter):
            conv_chunk(it * conv_per_iter + c)
        return carry

    lax.fori_loop(0, n_iter, middle, 0)

    y_a = _dot(za_buf[...], w_pool_proj[:, 0:D_MODEL])
    _, _, gate_ffn = split3(mf_b_ref)
    y_ref[...] = y_ref[...] + _unit_rms(f_buf[...].reshape(bt, tt, D_MODEL)) * (
        gate_ffn * g_post_ffn[...])
    conv = jnp.concatenate(
        [jnp.concatenate([vbuf[(bt + b) * conv_tiles + lt, 0:tt, :]
                          for lt in range(conv_tiles)], axis=1) for b in range(bt)], axis=0)
    mu = jnp.mean(conv, axis=-1, keepdims=True)
    xc = conv - mu
    var = jnp.mean(xc * xc, axis=-1, keepdims=True)
    ln = xc * lax.rsqrt(var + EPS) * ln_g[...] + ln_b[...]
    z_b = (ln * _sigmoid(ln)).astype(_BF16)
    off = D_POOL + 2 * D_CONV
    h = h_buf[...]
    g_a = _sigmoid(_dot(h, w_in[:, off:off + D_MODEL]))
    g_b = _sigmoid(_dot(h, w_in[:, off + D_MODEL:off + 2 * D_MODEL]))
    y_b = _dot(z_b, w_conv_proj[:, 0:D_MODEL])
    merged_carry[...] = (g_a * y_a + g_b * y_b).astype(_BF16)
    x_carry[...] = x_ref[...]

    @pl.when(jnp.logical_and(t_idx == nt - 1, step < n_tiles))
    def _write_state():
        for b in range(bt):
            for lt in range(pool_tiles):
                npool_ref[b, :, lt * LANE:(lt + 1) * LANE] = (
                    ubuf[b * pool_tiles + lt, POOL_PAD + tt - POOL_HIST:POOL_PAD + tt, :])
            for lt in range(conv_tiles):
                nconv_ref[b, :, lt * LANE:(lt + 1) * LANE] = (
                    vbuf[b * conv_tiles + lt, CONV_PAD + tt - CONV_HIST:CONV_PAD + tt, :])

    ubuf[:, 0:POOL_PAD, :] = ubuf[:, tt:tt + POOL_PAD, :]
    vbuf[0:bt * conv_tiles, 0:CONV_PAD, :] = vbuf[0:bt * conv_tiles, tt:tt + CONV_PAD, :]


def _tile_rows(batch, seq):
    rows = TILE_ROWS if batch * seq >= MIN_TILES_FULL * TILE_ROWS else TILE_ROWS // 2
    tt = min(seq, rows)
    bt = max(1, min(batch, rows // tt))
    return bt, tt


def _layer(x, mod_mix, mod_ffn, mod_row0, state, pos0, params):
    batch, seq, _ = x.shape
    bt, tt = _tile_rows(batch, seq)
    n_iter = D_FF // FF_CHUNK // FF_CHUNKS_PER_ITER
    assert batch % bt == 0 and seq % tt == 0 and tt % CONV_ROWS == 0 and tt >= CONV_PAD
    assert (bt * tt // CONV_ROWS) % n_iter == 0
    has_state = state is not None
    m = bt * tt
    nt = seq // tt
    n_tiles = (batch // bt) * nt

    def tile_a(s):
        return jnp.minimum(s, n_tiles - 1)

    def tile_b(s):
        return jnp.maximum(s - 1, 0)

    def per_batch(rows, cols, tile):
        return pl.BlockSpec((bt, rows, cols), lambda s: (tile(s) // nt, 0, 0))

    def per_tile(tile):
        return pl.BlockSpec((bt, tt, D_MODEL), lambda s: (tile(s) // nt, tile(s) % nt, 0))

    def whole(a):
        nd = a.ndim
        return pl.BlockSpec(a.shape, lambda s: (0,) * nd, pipeline_mode=pl.Buffered(1))

    assert mod_row0 % bt == 0

    def mod_spec(tile):
        return pl.BlockSpec((bt, 1, 3 * D_MODEL),
                            lambda s: (mod_row0 // bt + tile(s) // nt, 0, 0))

    in_specs = [per_tile(tile_a), mod_spec(tile_a), mod_spec(tile_b), mod_spec(tile_b)]
    args = [x, mod_mix, mod_mix, mod_ffn]
    if has_state:
        in_specs += [per_batch(POOL_HIST, D_POOL, tile_a), per_batch(CONV_HIST, D_CONV, tile_a)]
        args += list(state)
    in_specs += [whole(p) for p in params]
    args += list(params)

    return pl.pallas_call(
        functools.partial(_layer_body, bt=bt, tt=tt, nt=nt, n_tiles=n_tiles, pos0=pos0,
                          has_state=has_state),
        grid=(n_tiles + 1,),
        in_specs=in_specs,
        out_specs=[per_tile(tile_b), per_batch(POOL_HIST, D_POOL, tile_a),
                   per_batch(CONV_HIST, D_CONV, tile_a)],
        out_shape=[jax.ShapeDtypeStruct(x.shape, _F32),
                   jax.ShapeDtypeStruct((batch, POOL_HIST, D_POOL), _F32),
                   jax.ShapeDtypeStruct((batch, CONV_HIST, D_CONV), _F32)],
        scratch_shapes=[pltpu.VMEM((bt * D_POOL // LANE, POOL_PAD + tt, LANE), _F32),
                        pltpu.VMEM((2 * bt * D_CONV // LANE, CONV_PAD + tt, LANE), _F32),
                        pltpu.VMEM((bt, tt, D_MODEL), _F32),
                        pltpu.VMEM((m, D_MODEL), _BF16),
                        pltpu.VMEM((m, D_MODEL), _BF16),
                        pltpu.VMEM((m, D_MODEL), _BF16),
                        pltpu.VMEM((m, D_POOL), _BF16),
                        pltpu.VMEM((m, D_MODEL), _F32),
                        pltpu.VMEM((CONV_WIDTH, SUBLANE, D_CONV), _F32),
                        pltpu.VMEM((D_POOL // LANE, m, LANE), _F32)],
        compiler_params=pltpu.CompilerParams(
            dimension_semantics=("arbitrary",),
            vmem_limit_bytes=VMEM_LIMIT_BYTES),
        name="layer_state" if has_state else "layer",
    )(*args)


def kernel(x_prompt, x_sample, state_pool, state_conv, c_prompt, c_sample, w_ada_mix, b_ada_mix, g_pre_mix, g_post_mix, w_in, w_grp, pool_scale, w_pool_proj, w_dw, b_dw, ln_g, ln_b, w_conv_proj, w_out, w_ada_ffn, b_ada_ffn, g_pre_ffn, g_post_ffn, w_ff1, w_ff2):
    depth = w_in.shape[0]
    assert depth == 1
    l = 0
    bp = x_prompt.shape[0]
    bs = x_sample.shape[0]

    c_all = jnp.concatenate([c_prompt, c_sample], axis=0)
    mod_mix, mod_ffn = _adaln(c_all, w_ada_mix[l], b_ada_mix[l][None, :],
                              w_ada_ffn[l], b_ada_ffn[l][None, :])
    mod_mix = mod_mix[:, None, :]
    mod_ffn = mod_ffn[:, None, :]

    row = lambda a: a[l][None, :]
    params = (row(g_pre_mix), row(g_post_mix), w_in[l].astype(_BF16), w_grp[l].astype(_BF16),
              row(pool_scale), _pad_cols(w_pool_proj[l].astype(_BF16)), w_dw[l], row(b_dw),
              row(ln_g), row(ln_b), _pad_cols(w_conv_proj[l].astype(_BF16)),
              _pad_cols(w_out[l].astype(_BF16)), row(g_pre_ffn), row(g_post_ffn),
              _pad_cols(_chunked_cols(w_ff1[l].astype(_BF16), FF_CHUNK)),
              _pad_cols(w_ff2[l].astype(_BF16).reshape(D_FF // FF_CHUNK, FF_CHUNK, D_MODEL)))

    y_p, pool_p, conv_p = _layer(x_prompt, mod_mix, mod_ffn, 0, None, 0, params)
    y_s, pool_s, conv_s = _layer(x_sample, mod_mix, mod_ffn, bp,
                                 (state_pool[l], state_conv[l]), PAST_LEN, params)
    return (y_p, y_s, pool_p[None], conv_p[None], pool_s[None], conv_s[None])
```

```python
import functools

import jax
import jax.numpy as jnp
from jax import lax
from jax.experimental import pallas as pl
from jax.experimental.pallas import tpu as pltpu

D_MODEL = 1024
POOL_WINDOWS = (2, 4, 8, 16)
POOL_GROUP = 128
D_POOL = POOL_GROUP * len(POOL_WINDOWS)
POOL_HIST = max(POOL_WINDOWS) - 1
D_CONV = 512
CONV_WIDTH = 31
CONV_HIST = CONV_WIDTH - 1
D_FF = 4 * D_MODEL
EPS = 1e-6
PAST_LEN = 1024

LANE = 128
SUBLANE = 8
POOL_PAD = 16
CONV_PAD = 32
ROW_STRIDE = 4
CONV_ROWS = ROW_STRIDE * SUBLANE
FF_CHUNK = 1024
FF_CHUNKS_PER_ITER = 2
ADA_TILE = 512
TILE_ROWS = 512
MIN_TILES_FULL = 16
VMEM_LIMIT_BYTES = 60 * 1024 * 1024

_F32 = jnp.float32
_BF16 = jnp.bfloat16


def _dot(a, b):
    return jnp.dot(a, b, preferred_element_type=_F32)


NEG_LOG2_E = -1.4426950408889634


def _sigmoid(x):
    return 1.0 / (1.0 + jnp.exp2(x * NEG_LOG2_E))


def _unit_rms(x):
    return x * lax.rsqrt(jnp.mean(x * x, axis=-1, keepdims=True) + EPS)


def _adaln_body(c_ref, wm_ref, bm_ref, wf_ref, bf_ref, om_ref, of_ref):
    c = c_ref[...]
    s = (c * _sigmoid(c)).astype(_BF16)
    om_ref[...] = _dot(s, wm_ref[...].astype(_BF16)) + bm_ref[...]
    of_ref[...] = _dot(s, wf_ref[...].astype(_BF16)) + bf_ref[...]


def _adaln(c, w_mix, b_mix, w_ffn, b_ffn):
    n = c.shape[0]
    n_out = w_mix.shape[1]
    w_spec = pl.BlockSpec((D_MODEL, ADA_TILE), lambda j: (0, j))
    b_spec = pl.BlockSpec((1, ADA_TILE), lambda j: (0, j))
    o_spec = pl.BlockSpec((n, ADA_TILE), lambda j: (0, j))
    return pl.pallas_call(
        _adaln_body,
        grid=(n_out // ADA_TILE,),
        in_specs=[pl.BlockSpec((n, D_MODEL), lambda j: (0, 0)), w_spec, b_spec, w_spec, b_spec],
        out_specs=[o_spec, o_spec],
        out_shape=[jax.ShapeDtypeStruct((n, n_out), _F32)] * 2,
        name="adaln",
    )(c, w_mix, b_mix, w_ffn, b_ffn)


def _lane_major(w):
    k, n = w.shape
    return w.reshape(k, n // LANE, LANE).transpose(1, 0, 2)


def _cols(w_ref, first_tile, n_tiles, rows=None):
    if rows is None:
        parts = [w_ref[first_tile + q] for q in range(n_tiles)]
    else:
        parts = [w_ref[first_tile + q, rows, :] for q in range(n_tiles)]
    return jnp.concatenate(parts, axis=1)


def _layer_body(*refs, bt, tt, nt, n_tiles, pos0, has_state):
    refs = list(refs)
    x_ref, mm_a_ref, mm_b_ref, mf_b_ref = refs[:4]
    refs = refs[4:]
    if has_state:
        sp_ref, sc_ref = refs[:2]
        refs = refs[2:]
    (g_pre_mix, g_post_mix, w_in, w_grp, pool_scale, w_pool_proj, w_dw, b_dw, ln_g, ln_b,
     w_conv_proj, w_out, g_pre_ffn, g_post_ffn, w_ff1, w_ff2,
     y_ref, npool_ref, nconv_ref,
     ubuf, vbuf, x_carry, out_carry, h_buf, h2_buf, za_buf, f_buf, wdw_b, d_buf) = refs

    m = bt * tt
    step = pl.program_id(0)
    tile_a = jnp.minimum(step, n_tiles - 1)
    t_idx = tile_a % nt
    d_tiles = D_MODEL // LANE
    pool_tiles = D_POOL // LANE
    conv_tiles = D_CONV // LANE

    @pl.when(step == 0)
    def _init_carry():
        x_carry[...] = jnp.zeros((bt, tt, D_MODEL), _F32)
        out_carry[...] = jnp.zeros((m, D_MODEL), _F32)
        for j in range(CONV_WIDTH):
            wdw_b[j] = jnp.broadcast_to(w_dw[j:j + 1, :], (SUBLANE, D_CONV))

    @pl.when(t_idx == 0)
    def _init_history():
        ubuf[:, 0:POOL_PAD, :] = jnp.zeros((bt * pool_tiles, POOL_PAD, LANE), _F32)
        vbuf[0:bt * conv_tiles, 0:CONV_PAD, :] = jnp.zeros((bt * conv_tiles, CONV_PAD, LANE), _F32)
        if has_state:
            for b in range(bt):
                for lt in range(pool_tiles):
                    ubuf[b * pool_tiles + lt, POOL_PAD - POOL_HIST:POOL_PAD, :] = (
                        sp_ref[b, :, lt * LANE:(lt + 1) * LANE])
                for lt in range(conv_tiles):
                    vbuf[b * conv_tiles + lt, CONV_PAD - CONV_HIST:CONV_PAD, :] = (
                        sc_ref[b, :, lt * LANE:(lt + 1) * LANE])

    def split3(mod_ref):
        mod = mod_ref[...]
        return (mod[:, :, 0:D_MODEL], mod[:, :, D_MODEL:2 * D_MODEL],
                mod[:, :, 2 * D_MODEL:3 * D_MODEL])

    x = x_ref[...]
    shift, scale, _ = split3(mm_a_ref)
    h = (_unit_rms(x) * (g_pre_mix[...] * (1.0 + scale)) + shift).astype(_BF16).reshape(m, D_MODEL)
    h_buf[...] = h

    u = _dot(h, w_in[:, 0:D_POOL])
    for b in range(bt):
        for lt in range(pool_tiles):
            ubuf[b * pool_tiles + lt, POOL_PAD:POOL_PAD + tt, :] = (
                u[b * tt:(b + 1) * tt, lt * LANE:(lt + 1) * LANE])
    v = _dot(h, w_in[:, D_POOL:D_POOL + D_CONV]) * _sigmoid(
        _dot(h, w_in[:, D_POOL + D_CONV:D_POOL + 2 * D_CONV]))
    for b in range(bt):
        for lt in range(conv_tiles):
            vbuf[b * conv_tiles + lt, CONV_PAD:CONV_PAD + tt, :] = (
                v[b * tt:(b + 1) * tt, lt * LANE:(lt + 1) * LANE])
    _, _, gate_mix = split3(mm_b_ref)
    out = out_carry[...].reshape(bt, tt, D_MODEL)
    x1 = x_carry[...] + _unit_rms(out) * (gate_mix * g_post_mix[...])
    y_ref[...] = x1
    shift, scale, _ = split3(mf_b_ref)
    h2_buf[...] = (_unit_rms(x1) * (g_pre_ffn[...] * (1.0 + scale)) + shift).astype(
        _BF16).reshape(m, D_MODEL)

    pos_base = pos0 + t_idx * tt
    row_iota = ROW_STRIDE * lax.broadcasted_iota(jnp.int32, (SUBLANE, LANE), 0)
    for g, k in enumerate(POOL_WINDOWS):
        for b in range(bt):
            slab = b * pool_tiles + g
            for r0 in range(0, tt, CONV_ROWS):
                for rho in range(ROW_STRIDE):
                    first = POOL_PAD + r0 + rho
                    cur = ubuf[slab, pl.ds(first, SUBLANE, stride=ROW_STRIDE), :]
                    s = cur
                    for j in range(1, k):
                        s = s + ubuf[slab, pl.ds(first - j, SUBLANE, stride=ROW_STRIDE), :]
                    cnt = jnp.minimum(k, pos_base + (r0 + rho + 1) + row_iota).astype(_F32)
                    d_buf[g, pl.ds(b * tt + r0 + rho, SUBLANE, stride=ROW_STRIDE), :] = (
                        s / cnt - cur)
        lo, hi = g * POOL_GROUP, (g + 1) * POOL_GROUP
        za_buf[:, lo:hi] = (_dot(d_buf[g].astype(_BF16), w_grp[g])
                            * pool_scale[:, lo:hi]).astype(_BF16)

    n_ff = D_FF // FF_CHUNK
    n_iter = n_ff // FF_CHUNKS_PER_ITER
    ff_tiles = FF_CHUNK // LANE
    chunks_per_b = tt // CONV_ROWS
    conv_per_iter = bt * chunks_per_b // n_iter
    conv_base = CONV_PAD - CONV_HIST

    def conv_chunk(q):
        b = q // chunks_per_b
        r0 = (q % chunks_per_b) * CONV_ROWS
        for lt in range(conv_tiles):
            lanes = slice(lt * LANE, (lt + 1) * LANE)
            slab = b * conv_tiles + lt
            accs = [None] * ROW_STRIDE
            for o in range(CONV_WIDTH + ROW_STRIDE - 1):
                d_o = vbuf[slab, pl.ds(r0 + (conv_base + o), SUBLANE, stride=ROW_STRIDE), :]
                for rho in range(ROW_STRIDE):
                    j = o - rho
                    if 0 <= j < CONV_WIDTH:
                        term = wdw_b[j, :, lanes] * d_o
                        accs[rho] = term if accs[rho] is None else accs[rho] + term
            for rho in range(ROW_STRIDE):
                vbuf[bt * conv_tiles + slab, pl.ds(r0 + rho, SUBLANE, stride=ROW_STRIDE), :] = (
                    accs[rho] + b_dw[:, lanes])

    f_buf[...] = jnp.zeros((m, D_MODEL), _F32)

    def middle(it, carry):
        h2 = h2_buf[...]
        acts = []
        for c in range(FF_CHUNKS_PER_ITER):
            first = (it * FF_CHUNKS_PER_ITER + c) * ff_tiles
            a = jnp.maximum(_dot(h2, _cols(w_ff1, first, ff_tiles)), 0.0)
            acts.append((a * a).astype(_BF16))
        f = f_buf[...]
        for c in range(FF_CHUNKS_PER_ITER):
            rows = pl.ds(pl.multiple_of((it * FF_CHUNKS_PER_ITER + c) * FF_CHUNK, FF_CHUNK),
                         FF_CHUNK)
            f = f + _dot(acts[c], _cols(w_ff2, 0, d_tiles, rows))
        f_buf[...] = f
        for c in range(conv_per_iter):
            conv_chunk(it * conv_per_iter + c)
        return carry

    lax.fori_loop(0, n_iter, middle, 0)

    _, _, gate_ffn = split3(mf_b_ref)
    y_ref[...] = y_ref[...] + _unit_rms(f_buf[...].reshape(bt, tt, D_MODEL)) * (
        gate_ffn * g_post_ffn[...])
    conv = jnp.concatenate(
        [jnp.concatenate([vbuf[(bt + b) * conv_tiles + lt, 0:tt, :]
                          for lt in range(conv_tiles)], axis=1) for b in range(bt)], axis=0)
    mu = jnp.mean(conv, axis=-1, keepdims=True)
    xc = conv - mu
    var = jnp.mean(xc * xc, axis=-1, keepdims=True)
    ln = xc * lax.rsqrt(var + EPS) * ln_g[...] + ln_b[...]
    z_b = (ln * _sigmoid(ln)).astype(_BF16)
    off = D_POOL + 2 * D_CONV
    h = h_buf[...]
    z_a = za_buf[...]
    half = D_MODEL // 2
    half_tiles = d_tiles // 2
    out = None
    for hf in range(2):
        c0 = hf * half
        g_a = _sigmoid(_dot(h, w_in[:, off + c0:off + c0 + half]))
        g_b = _sigmoid(_dot(h, w_in[:, off + D_MODEL + c0:off + D_MODEL + c0 + half]))
        y_a = _dot(z_a, _cols(w_pool_proj, hf * half_tiles, half_tiles))
        y_b = _dot(z_b, _cols(w_conv_proj, hf * half_tiles, half_tiles))
        merged = (g_a * y_a + g_b * y_b).astype(_BF16)
        part = _dot(merged, _cols(w_out, 0, d_tiles, slice(c0, c0 + half)))
        out = part if out is None else out + part
    out_carry[...] = out
    x_carry[...] = x_ref[...]

    @pl.when(jnp.logical_and(t_idx == nt - 1, step < n_tiles))
    def _write_state():
        for b in range(bt):
            for lt in range(pool_tiles):
                npool_ref[b, :, lt * LANE:(lt + 1) * LANE] = (
                    ubuf[b * pool_tiles + lt, POOL_PAD + tt - POOL_HIST:POOL_PAD + tt, :])
            for lt in range(conv_tiles):
                nconv_ref[b, :, lt * LANE:(lt + 1) * LANE] = (
                    vbuf[b * conv_tiles + lt, CONV_PAD + tt - CONV_HIST:CONV_PAD + tt, :])

    ubuf[:, 0:POOL_PAD, :] = ubuf[:, tt:tt + POOL_PAD, :]
    vbuf[0:bt * conv_tiles, 0:CONV_PAD, :] = vbuf[0:bt * conv_tiles, tt:tt + CONV_PAD, :]


def _tile_rows(batch, seq):
    rows = TILE_ROWS if batch * seq >= MIN_TILES_FULL * TILE_ROWS else TILE_ROWS // 2
    tt = min(seq, rows)
    bt = max(1, min(batch, rows // tt))
    return bt, tt


def _layer(x, mod_mix, mod_ffn, state, pos0, params):
    batch, seq, _ = x.shape
    bt, tt = _tile_rows(batch, seq)
    n_iter = D_FF // FF_CHUNK // FF_CHUNKS_PER_ITER
    assert batch % bt == 0 and seq % tt == 0 and tt % CONV_ROWS == 0 and tt >= CONV_PAD
    assert (bt * tt // CONV_ROWS) % n_iter == 0
    has_state = state is not None
    m = bt * tt
    nt = seq // tt
    n_tiles = (batch // bt) * nt

    def tile_a(s):
        return jnp.minimum(s, n_tiles - 1)

    def tile_b(s):
        return jnp.maximum(s - 1, 0)

    def per_batch(rows, cols, tile):
        return pl.BlockSpec((bt, rows, cols), lambda s: (tile(s) // nt, 0, 0))

    def per_tile(tile):
        return pl.BlockSpec((bt, tt, D_MODEL), lambda s: (tile(s) // nt, tile(s) % nt, 0))

    def whole(a):
        nd = a.ndim
        return pl.BlockSpec(a.shape, lambda s: (0,) * nd, pipeline_mode=pl.Buffered(1))

    in_specs = [per_tile(tile_a), per_batch(1, 3 * D_MODEL, tile_a),
                per_batch(1, 3 * D_MODEL, tile_b), per_batch(1, 3 * D_MODEL, tile_b)]
    args = [x, mod_mix, mod_mix, mod_ffn]
    if has_state:
        in_specs += [per_batch(POOL_HIST, D_POOL, tile_a), per_batch(CONV_HIST, D_CONV, tile_a)]
        args += list(state)
    in_specs += [whole(p) for p in params]
    args += list(params)

    return pl.pallas_call(
        functools.partial(_layer_body, bt=bt, tt=tt, nt=nt, n_tiles=n_tiles, pos0=pos0,
                          has_state=has_state),
        grid=(n_tiles + 1,),
        in_specs=in_specs,
        out_specs=[per_tile(tile_b), per_batch(POOL_HIST, D_POOL, tile_a),
                   per_batch(CONV_HIST, D_CONV, tile_a)],
        out_shape=[jax.ShapeDtypeStruct(x.shape, _F32),
                   jax.ShapeDtypeStruct((batch, POOL_HIST, D_POOL), _F32),
                   jax.ShapeDtypeStruct((batch, CONV_HIST, D_CONV), _F32)],
        scratch_shapes=[pltpu.VMEM((bt * D_POOL // LANE, POOL_PAD + tt, LANE), _F32),
                        pltpu.VMEM((2 * bt * D_CONV // LANE, CONV_PAD + tt, LANE), _F32),
                        pltpu.VMEM((bt, tt, D_MODEL), _F32),
                        pltpu.VMEM((m, D_MODEL), _F32),
                        pltpu.VMEM((m, D_MODEL), _BF16),
                        pltpu.VMEM((m, D_MODEL), _BF16),
                        pltpu.VMEM((m, D_POOL), _BF16),
                        pltpu.VMEM((m, D_MODEL), _F32),
                        pltpu.VMEM((CONV_WIDTH, SUBLANE, D_CONV), _F32),
                        pltpu.VMEM((D_POOL // LANE, m, LANE), _F32)],
        compiler_params=pltpu.CompilerParams(
            dimension_semantics=("arbitrary",),
            vmem_limit_bytes=VMEM_LIMIT_BYTES),
        name="layer_state" if has_state else "layer",
    )(*args)


def kernel(x_prompt, x_sample, state_pool, state_conv, c_prompt, c_sample, w_ada_mix, b_ada_mix, g_pre_mix, g_post_mix, w_in, w_grp, pool_scale, w_pool_proj, w_dw, b_dw, ln_g, ln_b, w_conv_proj, w_out, w_ada_ffn, b_ada_ffn, g_pre_ffn, g_post_ffn, w_ff1, w_ff2):
    depth = w_in.shape[0]
    assert depth == 1
    l = 0
    bp = x_prompt.shape[0]

    c_all = jnp.concatenate([c_prompt, c_sample], axis=0)
    mod_mix, mod_ffn = _adaln(c_all, w_ada_mix[l], b_ada_mix[l][None, :],
                              w_ada_ffn[l], b_ada_ffn[l][None, :])
    mod_mix = mod_mix[:, None, :]
    mod_ffn = mod_ffn[:, None, :]

    row = lambda a: a[l][None, :]
    params = (row(g_pre_mix), row(g_post_mix), w_in[l].astype(_BF16), w_grp[l].astype(_BF16),
              row(pool_scale), _lane_major(w_pool_proj[l].astype(_BF16)), w_dw[l], row(b_dw),
              row(ln_g), row(ln_b), _lane_major(w_conv_proj[l].astype(_BF16)),
              _lane_major(w_out[l].astype(_BF16)), row(g_pre_ffn), row(g_post_ffn),
              _lane_major(w_ff1[l].astype(_BF16)), _lane_major(w_ff2[l].astype(_BF16)))

    y_p, pool_p, conv_p = _layer(x_prompt, mod_mix[:bp], mod_ffn[:bp], None, 0, params)
    y_s, pool_s, conv_s = _layer(x_sample, mod_mix[bp:], mod_ffn[bp:],
                                 (state_pool[l], state_conv[l]), PAST_LEN, params)
    return (y_p, y_s, pool_p[None], conv_p[None], pool_s[None], conv_s[None])
```

```python
import functools

import jax
import jax.numpy as jnp
from jax import lax
from jax.experimental import pallas as pl
from jax.experimental.pallas import tpu as pltpu

D_MODEL = 1024
POOL_WINDOWS = (2, 4, 8, 16)
POOL_GROUP = 128
D_POOL = POOL_GROUP * len(POOL_WINDOWS)
POOL_HIST = max(POOL_WINDOWS) - 1
D_CONV = 512
CONV_WIDTH = 31
CONV_HIST = CONV_WIDTH - 1
D_FF = 4 * D_MODEL
EPS = 1e-6
PAST_LEN = 1024

LANE = 128
SUBLANE = 8
POOL_PAD = 16
CONV_PAD = 32
ROW_STRIDE = 4
CONV_ROWS = ROW_STRIDE * SUBLANE
FF_CHUNK = 1024
FF_CHUNKS_PER_ITER = 2
ADA_TILE = 512
TILE_ROWS = 512
MIN_TILES_FULL = 16
VMEM_LIMIT_BYTES = 60 * 1024 * 1024

_F32 = jnp.float32
_BF16 = jnp.bfloat16


def _dot(a, b):
    return jnp.dot(a, b, preferred_element_type=_F32)


NEG_LOG2_E = -1.4426950408889634


def _sigmoid(x):
    return 1.0 / (1.0 + jnp.exp2(x * NEG_LOG2_E))


def _unit_rms(x):
    return x * lax.rsqrt(jnp.mean(x * x, axis=-1, keepdims=True) + EPS)


def _adaln_body(c_ref, wm_ref, bm_ref, wf_ref, bf_ref, om_ref, of_ref):
    c = c_ref[...]
    s = (c * _sigmoid(c)).astype(_BF16)
    om_ref[...] = _dot(s, wm_ref[...].astype(_BF16)) + bm_ref[...]
    of_ref[...] = _dot(s, wf_ref[...].astype(_BF16)) + bf_ref[...]


def _adaln(c, w_mix, b_mix, w_ffn, b_ffn):
    n = c.shape[0]
    n_out = w_mix.shape[1]
    w_spec = pl.BlockSpec((D_MODEL, ADA_TILE), lambda j: (0, j))
    b_spec = pl.BlockSpec((1, ADA_TILE), lambda j: (0, j))
    o_spec = pl.BlockSpec((n, ADA_TILE), lambda j: (0, j))
    return pl.pallas_call(
        _adaln_body,
        grid=(n_out // ADA_TILE,),
        in_specs=[pl.BlockSpec((n, D_MODEL), lambda j: (0, 0)), w_spec, b_spec, w_spec, b_spec],
        out_specs=[o_spec, o_spec],
        out_shape=[jax.ShapeDtypeStruct((n, n_out), _F32)] * 2,
        name="adaln",
    )(c, w_mix, b_mix, w_ffn, b_ffn)


def _lane_major(w):
    k, n = w.shape
    return w.reshape(k, n // LANE, LANE).transpose(1, 0, 2)


def _cols(w_ref, first_tile, n_tiles, rows=None):
    if rows is None:
        parts = [w_ref[first_tile + q] for q in range(n_tiles)]
    else:
        parts = [w_ref[first_tile + q, rows, :] for q in range(n_tiles)]
    return jnp.concatenate(parts, axis=1)


def _layer_body(*refs, bt, tt, nt, n_tiles, pos0, has_state):
    refs = list(refs)
    x_ref, xb_ref, mm_a_ref, mm_b_ref, mf_b_ref = refs[:5]
    refs = refs[5:]
    if has_state:
        sp_ref, sc_ref = refs[:2]
        refs = refs[2:]
    (g_pre_mix, g_post_mix, w_in, w_grp, pool_scale, w_pool_proj, w_dw, b_dw, ln_g, ln_b,
     w_conv_proj, w_out, g_pre_ffn, g_post_ffn, w_ff1, w_ff2,
     y_ref, npool_ref, nconv_ref,
     ubuf, vbuf, out_carry, h_buf, h2_buf, za_buf, f_buf, wdw_b, d_buf) = refs

    m = bt * tt
    step = pl.program_id(0)
    tile_a = jnp.minimum(step, n_tiles - 1)
    t_idx = tile_a % nt
    d_tiles = D_MODEL // LANE
    pool_tiles = D_POOL // LANE
    conv_tiles = D_CONV // LANE

    @pl.when(step == 0)
    def _init_carry():
        out_carry[...] = jnp.zeros((m, D_MODEL), _F32)
        for j in range(CONV_WIDTH):
            wdw_b[j] = jnp.broadcast_to(w_dw[j:j + 1, :], (SUBLANE, D_CONV))

    @pl.when(t_idx == 0)
    def _init_history():
        ubuf[:, 0:POOL_PAD, :] = jnp.zeros((bt * pool_tiles, POOL_PAD, LANE), _F32)
        vbuf[0:bt * conv_tiles, 0:CONV_PAD, :] = jnp.zeros((bt * conv_tiles, CONV_PAD, LANE), _F32)
        if has_state:
            for b in range(bt):
                for lt in range(pool_tiles):
                    ubuf[b * pool_tiles + lt, POOL_PAD - POOL_HIST:POOL_PAD, :] = (
                        sp_ref[b, :, lt * LANE:(lt + 1) * LANE])
                for lt in range(conv_tiles):
                    vbuf[b * conv_tiles + lt, CONV_PAD - CONV_HIST:CONV_PAD, :] = (
                        sc_ref[b, :, lt * LANE:(lt + 1) * LANE])

    def split3(mod_ref):
        mod = mod_ref[...]
        return (mod[:, :, 0:D_MODEL], mod[:, :, D_MODEL:2 * D_MODEL],
                mod[:, :, 2 * D_MODEL:3 * D_MODEL])

    x = x_ref[...]
    shift, scale, _ = split3(mm_a_ref)
    h = (_unit_rms(x) * (g_pre_mix[...] * (1.0 + scale)) + shift).astype(_BF16).reshape(m, D_MODEL)
    h_buf[...] = h

    u = _dot(h, w_in[:, 0:D_POOL])
    for b in range(bt):
        for lt in range(pool_tiles):
            ubuf[b * pool_tiles + lt, POOL_PAD:POOL_PAD + tt, :] = (
                u[b * tt:(b + 1) * tt, lt * LANE:(lt + 1) * LANE])
    v = _dot(h, w_in[:, D_POOL:D_POOL + D_CONV]) * _sigmoid(
        _dot(h, w_in[:, D_POOL + D_CONV:D_POOL + 2 * D_CONV]))
    for b in range(bt):
        for lt in range(conv_tiles):
            vbuf[b * conv_tiles + lt, CONV_PAD:CONV_PAD + tt, :] = (
                v[b * tt:(b + 1) * tt, lt * LANE:(lt + 1) * LANE])
    _, _, gate_mix = split3(mm_b_ref)
    out = out_carry[...].reshape(bt, tt, D_MODEL)
    x1 = xb_ref[...] + _unit_rms(out) * (gate_mix * g_post_mix[...])
    y_ref[...] = x1
    shift, scale, _ = split3(mf_b_ref)
    h2_buf[...] = (_unit_rms(x1) * (g_pre_ffn[...] * (1.0 + scale)) + shift).astype(
        _BF16).reshape(m, D_MODEL)

    pos_base = pos0 + t_idx * tt
    row_iota = ROW_STRIDE * lax.broadcasted_iota(jnp.int32, (SUBLANE, LANE), 0)
    for g, k in enumerate(POOL_WINDOWS):
        for b in range(bt):
            slab = b * pool_tiles + g
            for r0 in range(0, tt, CONV_ROWS):
                for rho in range(ROW_STRIDE):
                    first = POOL_PAD + r0 + rho
                    cur = ubuf[slab, pl.ds(first, SUBLANE, stride=ROW_STRIDE), :]
                    s = cur
                    for j in range(1, k):
                        s = s + ubuf[slab, pl.ds(first - j, SUBLANE, stride=ROW_STRIDE), :]
                    cnt = jnp.minimum(k, pos_base + (r0 + rho + 1) + row_iota).astype(_F32)
                    d_buf[g, pl.ds(b * tt + r0 + rho, SUBLANE, stride=ROW_STRIDE), :] = (
                        s / cnt - cur)
        lo, hi = g * POOL_GROUP, (g + 1) * POOL_GROUP
        za_buf[:, lo:hi] = (_dot(d_buf[g].astype(_BF16), w_grp[g])
                            * pool_scale[:, lo:hi]).astype(_BF16)

    n_ff = D_FF // FF_CHUNK
    n_iter = n_ff // FF_CHUNKS_PER_ITER
    ff_tiles = FF_CHUNK // LANE
    chunks_per_b = tt // CONV_ROWS
    conv_per_iter = bt * chunks_per_b // n_iter
    conv_base = CONV_PAD - CONV_HIST

    def conv_chunk(q):
        b = q // chunks_per_b
        r0 = (q % chunks_per_b) * CONV_ROWS
        for lt in range(conv_tiles):
            lanes = slice(lt * LANE, (lt + 1) * LANE)
            slab = b * conv_tiles + lt
            accs = [None] * ROW_STRIDE
            for o in range(CONV_WIDTH + ROW_STRIDE - 1):
                d_o = vbuf[slab, pl.ds(r0 + (conv_base + o), SUBLANE, stride=ROW_STRIDE), :]
                for rho in range(ROW_STRIDE):
                    j = o - rho
                    if 0 <= j < CONV_WIDTH:
                        term = wdw_b[j, :, lanes] * d_o
                        accs[rho] = term if accs[rho] is None else accs[rho] + term
            for rho in range(ROW_STRIDE):
                vbuf[bt * conv_tiles + slab, pl.ds(r0 + rho, SUBLANE, stride=ROW_STRIDE), :] = (
                    accs[rho] + b_dw[:, lanes])

    f_buf[...] = jnp.zeros((m, D_MODEL), _F32)

    def middle(it, carry):
        h2 = h2_buf[...]
        acts = []
        for c in range(FF_CHUNKS_PER_ITER):
            first = (it * FF_CHUNKS_PER_ITER + c) * ff_tiles
            a = jnp.maximum(_dot(h2, _cols(w_ff1, first, ff_tiles)), 0.0)
            acts.append((a * a).astype(_BF16))
        f = f_buf[...]
        for c in range(FF_CHUNKS_PER_ITER):
            rows = pl.ds(pl.multiple_of((it * FF_CHUNKS_PER_ITER + c) * FF_CHUNK, FF_CHUNK),
                         FF_CHUNK)
            f = f + _dot(acts[c], _cols(w_ff2, 0, d_tiles, rows))
        f_buf[...] = f
        for c in range(conv_per_iter):
            conv_chunk(it * conv_per_iter + c)
        return carry

    lax.fori_loop(0, n_iter, middle, 0)

    _, _, gate_ffn = split3(mf_b_ref)
    y_ref[...] = y_ref[...] + _unit_rms(f_buf[...].reshape(bt, tt, D_MODEL)) * (
        gate_ffn * g_post_ffn[...])
    conv = jnp.concatenate(
        [jnp.concatenate([vbuf[(bt + b) * conv_tiles + lt, 0:tt, :]
                          for lt in range(conv_tiles)], axis=1) for b in range(bt)], axis=0)
    mu = jnp.mean(conv, axis=-1, keepdims=True)
    xc = conv - mu
    var = jnp.mean(xc * xc, axis=-1, keepdims=True)
    ln = xc * lax.rsqrt(var + EPS) * ln_g[...] + ln_b[...]
    z_b = (ln * _sigmoid(ln)).astype(_BF16)
    off = D_POOL + 2 * D_CONV
    h = h_buf[...]
    z_a = za_buf[...]
    half = D_MODEL // 2
    half_tiles = d_tiles // 2
    out = None
    for hf in range(2):
        c0 = hf * half
        g_a = _sigmoid(_dot(h, w_in[:, off + c0:off + c0 + half]))
        g_b = _sigmoid(_dot(h, w_in[:, off + D_MODEL + c0:off + D_MODEL + c0 + half]))
        y_a = _dot(z_a, _cols(w_pool_proj, hf * half_tiles, half_tiles))
        y_b = _dot(z_b, _cols(w_conv_proj, hf * half_tiles, half_tiles))
        merged = (g_a * y_a + g_b * y_b).astype(_BF16)
        part = _dot(merged, _cols(w_out, 0, d_tiles, slice(c0, c0 + half)))
        out = part if out is None else out + part
    out_carry[...] = out

    @pl.when(jnp.logical_and(t_idx == nt - 1, step < n_tiles))
    def _write_state():
        for b in range(bt):
            for lt in range(pool_tiles):
                npool_ref[b, :, lt * LANE:(lt + 1) * LANE] = (
                    ubuf[b * pool_tiles + lt, POOL_PAD + tt - POOL_HIST:POOL_PAD + tt, :])
            for lt in range(conv_tiles):
                nconv_ref[b, :, lt * LANE:(lt + 1) * LANE] = (
                    vbuf[b * conv_tiles + lt, CONV_PAD + tt - CONV_HIST:CONV_PAD + tt, :])

    ubuf[:, 0:POOL_PAD, :] = ubuf[:, tt:tt + POOL_PAD, :]
    vbuf[0:bt * conv_tiles, 0:CONV_PAD, :] = vbuf[0:bt * conv_tiles, tt:tt + CONV_PAD, :]


def _tile_rows(batch, seq):
    rows = TILE_ROWS if batch * seq >= MIN_TILES_FULL * TILE_ROWS else TILE_ROWS // 2
    tt = min(seq, rows)
    bt = max(1, min(batch, rows // tt))
    return bt, tt


def _layer(x, mod_mix, mod_ffn, state, pos0, params):
    batch, seq, _ = x.shape
    bt, tt = _tile_rows(batch, seq)
    n_iter = D_FF // FF_CHUNK // FF_CHUNKS_PER_ITER
    assert batch % bt == 0 and seq % tt == 0 and tt % CONV_ROWS == 0 and tt >= CONV_PAD
    assert (bt * tt // CONV_ROWS) % n_iter == 0
    has_state = state is not None
    m = bt * tt
    nt = seq // tt
    n_tiles = (batch // bt) * nt

    def tile_a(s):
        return jnp.minimum(s, n_tiles - 1)

    def tile_b(s):
        return jnp.maximum(s - 1, 0)

    def per_batch(rows, cols, tile):
        return pl.BlockSpec((bt, rows, cols), lambda s: (tile(s) // nt, 0, 0))

    def per_tile(tile):
        return pl.BlockSpec((bt, tt, D_MODEL), lambda s: (tile(s) // nt, tile(s) % nt, 0))

    def whole(a):
        nd = a.ndim
        return pl.BlockSpec(a.shape, lambda s: (0,) * nd, pipeline_mode=pl.Buffered(1))

    in_specs = [per_tile(tile_a), per_tile(tile_b), per_batch(1, 3 * D_MODEL, tile_a),
                per_batch(1, 3 * D_MODEL, tile_b), per_batch(1, 3 * D_MODEL, tile_b)]
    args = [x, x, mod_mix, mod_mix, mod_ffn]
    if has_state:
        in_specs += [per_batch(POOL_HIST, D_POOL, tile_a), per_batch(CONV_HIST, D_CONV, tile_a)]
        args += list(state)
    in_specs += [whole(p) for p in params]
    args += list(params)

    return pl.pallas_call(
        functools.partial(_layer_body, bt=bt, tt=tt, nt=nt, n_tiles=n_tiles, pos0=pos0,
                          has_state=has_state),
        grid=(n_tiles + 1,),
        in_specs=in_specs,
        out_specs=[per_tile(tile_b), per_batch(POOL_HIST, D_POOL, tile_a),
                   per_batch(CONV_HIST, D_CONV, tile_a)],
        out_shape=[jax.ShapeDtypeStruct(x.shape, _F32),
                   jax.ShapeDtypeStruct((batch, POOL_HIST, D_POOL), _F32),
                   jax.ShapeDtypeStruct((batch, CONV_HIST, D_CONV), _F32)],
        scratch_shapes=[pltpu.VMEM((bt * D_POOL // LANE, POOL_PAD + tt, LANE), _F32),
                        pltpu.VMEM((2 * bt * D_CONV // LANE, CONV_PAD + tt, LANE), _F32),
                        pltpu.VMEM((m, D_MODEL), _F32),
                        pltpu.VMEM((m, D_MODEL), _BF16),
                        pltpu.VMEM((m, D_MODEL), _BF16),
                        pltpu.VMEM((m, D_POOL), _BF16),
                        pltpu.VMEM((m, D_MODEL), _F32),
                        pltpu.VMEM((CONV_WIDTH, SUBLANE, D_CONV), _F32),
                        pltpu.VMEM((D_POOL // LANE, m, LANE), _F32)],
        compiler_params=pltpu.CompilerParams(
            dimension_semantics=("arbitrary",),
            vmem_limit_bytes=VMEM_LIMIT_BYTES),
        name="layer_state" if has_state else "layer",
    )(*args)


def kernel(x_prompt, x_sample, state_pool, state_conv, c_prompt, c_sample, w_ada_mix, b_ada_mix, g_pre_mix, g_post_mix, w_in, w_grp, pool_scale, w_pool_proj, w_dw, b_dw, ln_g, ln_b, w_conv_proj, w_out, w_ada_ffn, b_ada_ffn, g_pre_ffn, g_post_ffn, w_ff1, w_ff2):
    depth = w_in.shape[0]
    assert depth == 1
    l = 0
    bp = x_prompt.shape[0]

    c_all = jnp.concatenate([c_prompt, c_sample], axis=0)
    mod_mix, mod_ffn = _adaln(c_all, w_ada_mix[l], b_ada_mix[l][None, :],
                              w_ada_ffn[l], b_ada_ffn[l][None, :])
    mod_mix = mod_mix[:, None, :]
    mod_ffn = mod_ffn[:, None, :]

    row = lambda a: a[l][None, :]
    params = (row(g_pre_mix), row(g_post_mix), w_in[l].astype(_BF16), w_grp[l].astype(_BF16),
              row(pool_scale), _lane_major(w_pool_proj[l].astype(_BF16)), w_dw[l], row(b_dw),
              row(ln_g), row(ln_b), _lane_major(w_conv_proj[l].astype(_BF16)),
              _lane_major(w_out[l].astype(_BF16)), row(g_pre_ffn), row(g_post_ffn),
              _lane_major(w_ff1[l].astype(_BF16)), _lane_major(w_ff2[l].astype(_BF16)))

    y_p, pool_p, conv_p = _layer(x_prompt, mod_mix[:bp], mod_ffn[:bp], None, 0, params)
    y_s, pool_s, conv_s = _layer(x_sample, mod_mix[bp:], mod_ffn[bp:],
                                 (state_pool[l], state_conv[l]), PAST_LEN, params)
    return (y_p, y_s, pool_p[None], conv_p[None], pool_s[None], conv_s[None])
```

```python
import functools

import jax
import jax.numpy as jnp
from jax import lax
from jax.experimental import pallas as pl
from jax.experimental.pallas import tpu as pltpu

D_MODEL = 1024
POOL_WINDOWS = (2, 4, 8, 16)
POOL_GROUP = 128
D_POOL = POOL_GROUP * len(POOL_WINDOWS)
POOL_HIST = max(POOL_WINDOWS) - 1
D_CONV = 512
CONV_WIDTH = 31
CONV_HIST = CONV_WIDTH - 1
D_FF = 4 * D_MODEL
EPS = 1e-6
PAST_LEN = 1024

LANE = 128
SUBLANE = 8
POOL_PAD = 16
CONV_PAD = 32
ROW_STRIDE = 4
CONV_ROWS = ROW_STRIDE * SUBLANE
FF_CHUNK = 1024
FF_CHUNKS_PER_ITER = 2
ADA_TILE = 512
TILE_ROWS = 512
MIN_TILES_FULL = 16
VMEM_LIMIT_BYTES = 60 * 1024 * 1024

_F32 = jnp.float32
_BF16 = jnp.bfloat16


def _dot(a, b):
    return jnp.dot(a, b, preferred_element_type=_F32)


NEG_LOG2_E = -1.4426950408889634


def _sigmoid(x):
    return 1.0 / (1.0 + jnp.exp2(x * NEG_LOG2_E))


def _unit_rms(x):
    return x * lax.rsqrt(jnp.mean(x * x, axis=-1, keepdims=True) + EPS)


def _adaln_body(c_ref, wm_ref, bm_ref, wf_ref, bf_ref, om_ref, of_ref):
    c = c_ref[...]
    s = (c * _sigmoid(c)).astype(_BF16)
    om_ref[...] = _dot(s, wm_ref[...].astype(_BF16)) + bm_ref[...]
    of_ref[...] = _dot(s, wf_ref[...].astype(_BF16)) + bf_ref[...]


def _adaln(c, w_mix, b_mix, w_ffn, b_ffn):
    n = c.shape[0]
    n_out = w_mix.shape[1]
    w_spec = pl.BlockSpec((D_MODEL, ADA_TILE), lambda j: (0, j))
    b_spec = pl.BlockSpec((1, ADA_TILE), lambda j: (0, j))
    o_spec = pl.BlockSpec((n, ADA_TILE), lambda j: (0, j))
    return pl.pallas_call(
        _adaln_body,
        grid=(n_out // ADA_TILE,),
        in_specs=[pl.BlockSpec((n, D_MODEL), lambda j: (0, 0)), w_spec, b_spec, w_spec, b_spec],
        out_specs=[o_spec, o_spec],
        out_shape=[jax.ShapeDtypeStruct((n, n_out), _F32)] * 2,
        name="adaln",
    )(c, w_mix, b_mix, w_ffn, b_ffn)


def _lane_major(w):
    k, n = w.shape
    return w.reshape(k, n // LANE, LANE).transpose(1, 0, 2)


def _cols(w_ref, first_tile, n_tiles, rows=None):
    if rows is None:
        parts = [w_ref[first_tile + q] for q in range(n_tiles)]
    else:
        parts = [w_ref[first_tile + q, rows, :] for q in range(n_tiles)]
    return jnp.concatenate(parts, axis=1)


def _layer_body(*refs, bt, tt, nt, n_tiles, pos0, has_state):
    refs = list(refs)
    x_ref, xb_ref, mm_a_ref, mm_b_ref, mf_b_ref = refs[:5]
    refs = refs[5:]
    if has_state:
        sp_ref, sc_ref = refs[:2]
        refs = refs[2:]
    (g_pre_mix, g_post_mix, w_in, w_grp, pool_scale, w_pool_proj, w_dw, b_dw, ln_g, ln_b,
     w_conv_proj, w_out, g_pre_ffn, g_post_ffn, w_ff1, w_ff2,
     y_ref, npool_ref, nconv_ref,
     ubuf, vbuf, out_carry, h_buf, h2_buf, za_buf, f_buf, wdw_b, d_buf) = refs

    m = bt * tt
    step = pl.program_id(0)
    tile_a = jnp.minimum(step, n_tiles - 1)
    t_idx = tile_a % nt
    d_tiles = D_MODEL // LANE
    pool_tiles = D_POOL // LANE
    conv_tiles = D_CONV // LANE

    @pl.when(step == 0)
    def _init_carry():
        out_carry[...] = jnp.zeros((m, D_MODEL), _F32)
        for j in range(CONV_WIDTH):
            wdw_b[j] = jnp.broadcast_to(w_dw[j:j + 1, :], (SUBLANE, D_CONV))

    @pl.when(t_idx == 0)
    def _init_history():
        ubuf[:, 0:POOL_PAD, :] = jnp.zeros((bt * pool_tiles, POOL_PAD, LANE), _F32)
        vbuf[0:bt * conv_tiles, 0:CONV_PAD, :] = jnp.zeros((bt * conv_tiles, CONV_PAD, LANE), _F32)
        if has_state:
            for b in range(bt):
                for lt in range(pool_tiles):
                    ubuf[b * pool_tiles + lt, POOL_PAD - POOL_HIST:POOL_PAD, :] = (
                        sp_ref[b, :, lt * LANE:(lt + 1) * LANE])
                for lt in range(conv_tiles):
                    vbuf[b * conv_tiles + lt, CONV_PAD - CONV_HIST:CONV_PAD, :] = (
                        sc_ref[b, :, lt * LANE:(lt + 1) * LANE])

    def split3(mod_ref):
        mod = mod_ref[...]
        return (mod[:, :, 0:D_MODEL], mod[:, :, D_MODEL:2 * D_MODEL],
                mod[:, :, 2 * D_MODEL:3 * D_MODEL])

    x = x_ref[...]
    shift, scale, _ = split3(mm_a_ref)
    h = (_unit_rms(x) * (g_pre_mix[...] * (1.0 + scale)) + shift).astype(_BF16).reshape(m, D_MODEL)
    h_buf[...] = h

    u = _dot(h, w_in[:, 0:D_POOL])
    for b in range(bt):
        for lt in range(pool_tiles):
            ubuf[b * pool_tiles + lt, POOL_PAD:POOL_PAD + tt, :] = (
                u[b * tt:(b + 1) * tt, lt * LANE:(lt + 1) * LANE])
    v = _dot(h, w_in[:, D_POOL:D_POOL + D_CONV]) * _sigmoid(
        _dot(h, w_in[:, D_POOL + D_CONV:D_POOL + 2 * D_CONV]))
    for b in range(bt):
        for lt in range(conv_tiles):
            vbuf[b * conv_tiles + lt, CONV_PAD:CONV_PAD + tt, :] = (
                v[b * tt:(b + 1) * tt, lt * LANE:(lt + 1) * LANE])
    _, _, gate_mix = split3(mm_b_ref)
    out = out_carry[...].reshape(bt, tt, D_MODEL)
    x1 = xb_ref[...] + _unit_rms(out) * (gate_mix * g_post_mix[...])
    y_ref[...] = x1
    shift, scale, _ = split3(mf_b_ref)
    h2_buf[...] = (_unit_rms(x1) * (g_pre_ffn[...] * (1.0 + scale)) + shift).astype(
        _BF16).reshape(m, D_MODEL)

    pos_base = pos0 + t_idx * tt
    row_iota = ROW_STRIDE * lax.broadcasted_iota(jnp.int32, (SUBLANE, LANE), 0)
    for g, k in enumerate(POOL_WINDOWS):
        for b in range(bt):
            slab = b * pool_tiles + g
            for r0 in range(0, tt, CONV_ROWS):
                for rho in range(ROW_STRIDE):
                    first = POOL_PAD + r0 + rho
                    cur = ubuf[slab, pl.ds(first, SUBLANE, stride=ROW_STRIDE), :]
                    s = cur
                    for j in range(1, k):
                        s = s + ubuf[slab, pl.ds(first - j, SUBLANE, stride=ROW_STRIDE), :]
                    cnt = jnp.minimum(k, pos_base + (r0 + rho + 1) + row_iota).astype(_F32)
                    d_buf[g, pl.ds(b * tt + r0 + rho, SUBLANE, stride=ROW_STRIDE), :] = (
                        s / cnt - cur)
        lo, hi = g * POOL_GROUP, (g + 1) * POOL_GROUP
        za_buf[:, lo:hi] = (_dot(d_buf[g].astype(_BF16), w_grp[g])
                            * pool_scale[:, lo:hi]).astype(_BF16)

    n_ff = D_FF // FF_CHUNK
    n_iter = n_ff // FF_CHUNKS_PER_ITER
    ff_tiles = FF_CHUNK // LANE
    chunks_per_b = tt // CONV_ROWS
    conv_per_iter = bt * chunks_per_b // n_iter
    conv_base = CONV_PAD - CONV_HIST

    def conv_chunk(q):
        b = q // chunks_per_b
        r0 = (q % chunks_per_b) * CONV_ROWS
        for lt in range(conv_tiles):
            lanes = slice(lt * LANE, (lt + 1) * LANE)
            slab = b * conv_tiles + lt
            accs = [None] * ROW_STRIDE
            for o in range(CONV_WIDTH + ROW_STRIDE - 1):
                d_o = vbuf[slab, pl.ds(r0 + (conv_base + o), SUBLANE, stride=ROW_STRIDE), :]
                for rho in range(ROW_STRIDE):
                    j = o - rho
                    if 0 <= j < CONV_WIDTH:
                        term = wdw_b[j, :, lanes] * d_o
                        accs[rho] = term if accs[rho] is None else accs[rho] + term
            for rho in range(ROW_STRIDE):
                vbuf[bt * conv_tiles + slab, pl.ds(r0 + rho, SUBLANE, stride=ROW_STRIDE), :] = (
                    accs[rho] + b_dw[:, lanes])

    f_buf[...] = jnp.zeros((m, D_MODEL), _F32)

    def middle(it, carry):
        h2 = h2_buf[...]
        acts = []
        for c in range(FF_CHUNKS_PER_ITER):
            first = (it * FF_CHUNKS_PER_ITER + c) * ff_tiles
            a = jnp.maximum(_dot(h2, _cols(w_ff1, first, ff_tiles)), 0.0)
            acts.append((a * a).astype(_BF16))
        f = f_buf[...]
        for c in range(FF_CHUNKS_PER_ITER):
            row0 = (it * FF_CHUNKS_PER_ITER + c) * FF_CHUNK
            f = f + _dot(acts[c], _cols(w_ff2, 0, d_tiles, slice(row0, row0 + FF_CHUNK)))
        f_buf[...] = f
        for c in range(conv_per_iter):
            conv_chunk(it * conv_per_iter + c)
        return carry

    for it in range(n_iter):
        middle(it, 0)

    _, _, gate_ffn = split3(mf_b_ref)
    y_ref[...] = y_ref[...] + _unit_rms(f_buf[...].reshape(bt, tt, D_MODEL)) * (
        gate_ffn * g_post_ffn[...])
    conv = jnp.concatenate(
        [jnp.concatenate([vbuf[(bt + b) * conv_tiles + lt, 0:tt, :]
                          for lt in range(conv_tiles)], axis=1) for b in range(bt)], axis=0)
    mu = jnp.mean(conv, axis=-1, keepdims=True)
    xc = conv - mu
    var = jnp.mean(xc * xc, axis=-1, keepdims=True)
    ln = xc * lax.rsqrt(var + EPS) * ln_g[...] + ln_b[...]
    z_b = (ln * _sigmoid(ln)).astype(_BF16)
    off = D_POOL + 2 * D_CONV
    h = h_buf[...]
    z_a = za_buf[...]
    half = D_MODEL // 2
    half_tiles = d_tiles // 2
    out = None
    for hf in range(2):
        c0 = hf * half
        g_a = _sigmoid(_dot(h, w_in[:, off + c0:off + c0 + half]))
        g_b = _sigmoid(_dot(h, w_in[:, off + D_MODEL + c0:off + D_MODEL + c0 + half]))
        y_a = _dot(z_a, _cols(w_pool_proj, hf * half_tiles, half_tiles))
        y_b = _dot(z_b, _cols(w_conv_proj, hf * half_tiles, half_tiles))
        merged = (g_a * y_a + g_b * y_b).astype(_BF16)
        part = _dot(merged, _cols(w_out, 0, d_tiles, slice(c0, c0 + half)))
        out = part if out is None else out + part
    out_carry[...] = out

    @pl.when(jnp.logical_and(t_idx == nt - 1, step < n_tiles))
    def _write_state():
        for b in range(bt):
            for lt in range(pool_tiles):
                npool_ref[b, :, lt * LANE:(lt + 1) * LANE] = (
                    ubuf[b * pool_tiles + lt, POOL_PAD + tt - POOL_HIST:POOL_PAD + tt, :])
            for lt in range(conv_tiles):
                nconv_ref[b, :, lt * LANE:(lt + 1) * LANE] = (
                    vbuf[b * conv_tiles + lt, CONV_PAD + tt - CONV_HIST:CONV_PAD + tt, :])

    ubuf[:, 0:POOL_PAD, :] = ubuf[:, tt:tt + POOL_PAD, :]
    vbuf[0:bt * conv_tiles, 0:CONV_PAD, :] = vbuf[0:bt * conv_tiles, tt:tt + CONV_PAD, :]


def _tile_rows(batch, seq):
    rows = TILE_ROWS if batch * seq >= MIN_TILES_FULL * TILE_ROWS else TILE_ROWS // 2
    tt = min(seq, rows)
    bt = max(1, min(batch, rows // tt))
    return bt, tt


def _layer(x, mod_mix, mod_ffn, state, pos0, params):
    batch, seq, _ = x.shape
    bt, tt = _tile_rows(batch, seq)
    n_iter = D_FF // FF_CHUNK // FF_CHUNKS_PER_ITER
    assert batch % bt == 0 and seq % tt == 0 and tt % CONV_ROWS == 0 and tt >= CONV_PAD
    assert (bt * tt // CONV_ROWS) % n_iter == 0
    has_state = state is not None
    m = bt * tt
    nt = seq // tt
    n_tiles = (batch // bt) * nt

    def tile_a(s):
        return jnp.minimum(s, n_tiles - 1)

    def tile_b(s):
        return jnp.maximum(s - 1, 0)

    def per_batch(rows, cols, tile):
        return pl.BlockSpec((bt, rows, cols), lambda s: (tile(s) // nt, 0, 0))

    def per_tile(tile):
        return pl.BlockSpec((bt, tt, D_MODEL), lambda s: (tile(s) // nt, tile(s) % nt, 0))

    def whole(a):
        nd = a.ndim
        return pl.BlockSpec(a.shape, lambda s: (0,) * nd, pipeline_mode=pl.Buffered(1))

    in_specs = [per_tile(tile_a), per_tile(tile_b), per_batch(1, 3 * D_MODEL, tile_a),
                per_batch(1, 3 * D_MODEL, tile_b), per_batch(1, 3 * D_MODEL, tile_b)]
    args = [x, x, mod_mix, mod_mix, mod_ffn]
    if has_state:
        in_specs += [per_batch(POOL_HIST, D_POOL, tile_a), per_batch(CONV_HIST, D_CONV, tile_a)]
        args += list(state)
    in_specs += [whole(p) for p in params]
    args += list(params)

    return pl.pallas_call(
        functools.partial(_layer_body, bt=bt, tt=tt, nt=nt, n_tiles=n_tiles, pos0=pos0,
                          has_state=has_state),
        grid=(n_tiles + 1,),
        in_specs=in_specs,
        out_specs=[per_tile(tile_b), per_batch(POOL_HIST, D_POOL, tile_a),
                   per_batch(CONV_HIST, D_CONV, tile_a)],
        out_shape=[jax.ShapeDtypeStruct(x.shape, _F32),
                   jax.ShapeDtypeStruct((batch, POOL_HIST, D_POOL), _F32),
                   jax.ShapeDtypeStruct((batch, CONV_HIST, D_CONV), _F32)],
        scratch_shapes=[pltpu.VMEM((bt * D_POOL // LANE, POOL_PAD + tt, LANE), _F32),
                        pltpu.VMEM((2 * bt * D_CONV // LANE, CONV_PAD + tt, LANE), _F32),
                        pltpu.VMEM((m, D_MODEL), _F32),
                        pltpu.VMEM((m, D_MODEL), _BF16),
                        pltpu.VMEM((m, D_MODEL), _BF16),
                        pltpu.VMEM((m, D_POOL), _BF16),
                        pltpu.VMEM((m, D_MODEL), _F32),
                        pltpu.VMEM((CONV_WIDTH, SUBLANE, D_CONV), _F32),
                        pltpu.VMEM((D_POOL // LANE, m, LANE), _F32)],
        compiler_params=pltpu.CompilerParams(
            dimension_semantics=("arbitrary",),
            vmem_limit_bytes=VMEM_LIMIT_BYTES),
        name="layer_state" if has_state else "layer",
    )(*args)


def kernel(x_prompt, x_sample, state_pool, state_conv, c_prompt, c_sample, w_ada_mix, b_ada_mix, g_pre_mix, g_post_mix, w_in, w_grp, pool_scale, w_pool_proj, w_dw, b_dw, ln_g, ln_b, w_conv_proj, w_out, w_ada_ffn, b_ada_ffn, g_pre_ffn, g_post_ffn, w_ff1, w_ff2):
    depth = w_in.shape[0]
    assert depth == 1
    l = 0
    bp = x_prompt.shape[0]

    c_all = jnp.concatenate([c_prompt, c_sample], axis=0)
    mod_mix, mod_ffn = _adaln(c_all, w_ada_mix[l], b_ada_mix[l][None, :],
                              w_ada_ffn[l], b_ada_ffn[l][None, :])
    mod_mix = mod_mix[:, None, :]
    mod_ffn = mod_ffn[:, None, :]

    row = lambda a: a[l][None, :]
    params = (row(g_pre_mix), row(g_post_mix), w_in[l].astype(_BF16), w_grp[l].astype(_BF16),
              row(pool_scale), _lane_major(w_pool_proj[l].astype(_BF16)), w_dw[l], row(b_dw),
              row(ln_g), row(ln_b), _lane_major(w_conv_proj[l].astype(_BF16)),
              _lane_major(w_out[l].astype(_BF16)), row(g_pre_ffn), row(g_post_ffn),
              _lane_major(w_ff1[l].astype(_BF16)), _lane_major(w_ff2[l].astype(_BF16)))

    y_p, pool_p, conv_p = _layer(x_prompt, mod_mix[:bp], mod_ffn[:bp], None, 0, params)
    y_s, pool_s, conv_s = _layer(x_sample, mod_mix[bp:], mod_ffn[bp:],
                                 (state_pool[l], state_conv[l]), PAST_LEN, params)
    return (y_p, y_s, pool_p[None], conv_p[None], pool_s[None], conv_s[None])
```

```python
import functools

import jax
import jax.numpy as jnp
from jax import lax
from jax.experimental import pallas as pl
from jax.experimental.pallas import tpu as pltpu

D_MODEL = 1024
POOL_WINDOWS = (2, 4, 8, 16)
POOL_GROUP = 128
D_POOL = POOL_GROUP * len(POOL_WINDOWS)
POOL_HIST = max(POOL_WINDOWS) - 1
D_CONV = 512
CONV_WIDTH = 31
CONV_HIST = CONV_WIDTH - 1
D_FF = 4 * D_MODEL
EPS = 1e-6
PAST_LEN = 1024

LANE = 128
SUBLANE = 8
POOL_PAD = 16
CONV_PAD = 32
ROW_STRIDE = 4
CONV_ROWS = ROW_STRIDE * SUBLANE
FF_CHUNK = 1024
FF_CHUNKS_PER_ITER = 2
ADA_TILE = 512
TILE_ROWS = 512
MIN_TILES_FULL = 16
VMEM_LIMIT_BYTES = 60 * 1024 * 1024

_F32 = jnp.float32
_BF16 = jnp.bfloat16


def _dot(a, b):
    return jnp.dot(a, b, preferred_element_type=_F32)


NEG_LOG2_E = -1.4426950408889634


def _sigmoid(x):
    return 1.0 / (1.0 + jnp.exp2(x * NEG_LOG2_E))


def _unit_rms(x):
    return x * lax.rsqrt(jnp.mean(x * x, axis=-1, keepdims=True) + EPS)


def _adaln_body(c_ref, wm_ref, bm_ref, wf_ref, bf_ref, om_ref, of_ref):
    c = c_ref[...]
    s = (c * _sigmoid(c)).astype(_BF16)
    om_ref[...] = _dot(s, wm_ref[...].astype(_BF16)) + bm_ref[...]
    of_ref[...] = _dot(s, wf_ref[...].astype(_BF16)) + bf_ref[...]


def _adaln(c, w_mix, b_mix, w_ffn, b_ffn):
    n = c.shape[0]
    n_out = w_mix.shape[1]
    w_spec = pl.BlockSpec((D_MODEL, ADA_TILE), lambda j: (0, j))
    b_spec = pl.BlockSpec((1, ADA_TILE), lambda j: (0, j))
    o_spec = pl.BlockSpec((n, ADA_TILE), lambda j: (0, j))
    return pl.pallas_call(
        _adaln_body,
        grid=(n_out // ADA_TILE,),
        in_specs=[pl.BlockSpec((n, D_MODEL), lambda j: (0, 0)), w_spec, b_spec, w_spec, b_spec],
        out_specs=[o_spec, o_spec],
        out_shape=[jax.ShapeDtypeStruct((n, n_out), _F32)] * 2,
        name="adaln",
    )(c, w_mix, b_mix, w_ffn, b_ffn)


def _lane_major(w):
    k, n = w.shape
    return w.reshape(k, n // LANE, LANE).transpose(1, 0, 2)


def _cols(w_ref, first_tile, n_tiles, rows=None):
    if rows is None:
        parts = [w_ref[first_tile + q] for q in range(n_tiles)]
    else:
        parts = [w_ref[first_tile + q, rows, :] for q in range(n_tiles)]
    return jnp.concatenate(parts, axis=1)


def _layer_body(*refs, bt, tt, nt, n_tiles, pos0, has_state):
    refs = list(refs)
    x_ref, xb_ref, mm_a_ref, mm_b_ref, mf_b_ref = refs[:5]
    refs = refs[5:]
    if has_state:
        sp_ref, sc_ref = refs[:2]
        refs = refs[2:]
    (g_pre_mix, g_post_mix, w_in, w_grp, pool_scale, w_pool_proj, w_dw, b_dw, ln_g, ln_b,
     w_conv_proj, w_out, g_pre_ffn, g_post_ffn, w_ff1, w_ff2,
     y_ref, npool_ref, nconv_ref,
     ubuf, vbuf, out_carry, h_buf, h2_buf, za_buf, f_buf, wdw_b, d_buf) = refs

    m = bt * tt
    step = pl.program_id(0)
    tile_a = jnp.minimum(step, n_tiles - 1)
    t_idx = tile_a % nt
    d_tiles = D_MODEL // LANE
    pool_tiles = D_POOL // LANE
    conv_tiles = D_CONV // LANE

    @pl.when(step == 0)
    def _init_carry():
        out_carry[...] = jnp.zeros((m, D_MODEL), _F32)
        for j in range(CONV_WIDTH):
            wdw_b[j] = jnp.broadcast_to(w_dw[j:j + 1, :], (SUBLANE, D_CONV))

    @pl.when(t_idx == 0)
    def _init_history():
        ubuf[:, 0:POOL_PAD, :] = jnp.zeros((bt * pool_tiles, POOL_PAD, LANE), _F32)
        vbuf[0:bt * conv_tiles, 0:CONV_PAD, :] = jnp.zeros((bt * conv_tiles, CONV_PAD, LANE), _F32)
        if has_state:
            for b in range(bt):
                for lt in range(pool_tiles):
                    ubuf[b * pool_tiles + lt, POOL_PAD - POOL_HIST:POOL_PAD, :] = (
                        sp_ref[b, :, lt * LANE:(lt + 1) * LANE])
                for lt in range(conv_tiles):
                    vbuf[b * conv_tiles + lt, CONV_PAD - CONV_HIST:CONV_PAD, :] = (
                        sc_ref[b, :, lt * LANE:(lt + 1) * LANE])

    def split3(mod_ref):
        mod = mod_ref[...]
        return (mod[:, :, 0:D_MODEL], mod[:, :, D_MODEL:2 * D_MODEL],
                mod[:, :, 2 * D_MODEL:3 * D_MODEL])

    x = x_ref[...]
    shift, scale, _ = split3(mm_a_ref)
    h = (_unit_rms(x) * (g_pre_mix[...] * (1.0 + scale)) + shift).astype(_BF16).reshape(m, D_MODEL)
    h_buf[...] = h

    u = _dot(h, w_in[:, 0:D_POOL])
    for b in range(bt):
        for lt in range(pool_tiles):
            ubuf[b * pool_tiles + lt, POOL_PAD:POOL_PAD + tt, :] = (
                u[b * tt:(b + 1) * tt, lt * LANE:(lt + 1) * LANE])
    v = _dot(h, w_in[:, D_POOL:D_POOL + D_CONV]) * _sigmoid(
        _dot(h, w_in[:, D_POOL + D_CONV:D_POOL + 2 * D_CONV]))
    for b in range(bt):
        for lt in range(conv_tiles):
            vbuf[b * conv_tiles + lt, CONV_PAD:CONV_PAD + tt, :] = (
                v[b * tt:(b + 1) * tt, lt * LANE:(lt + 1) * LANE])
    _, _, gate_mix = split3(mm_b_ref)
    out = out_carry[...].reshape(bt, tt, D_MODEL)
    x1 = xb_ref[...] + _unit_rms(out) * (gate_mix * g_post_mix[...])
    y_ref[...] = x1
    shift, scale, _ = split3(mf_b_ref)
    h2_buf[...] = (_unit_rms(x1) * (g_pre_ffn[...] * (1.0 + scale)) + shift).astype(
        _BF16).reshape(m, D_MODEL)

    pos_base = pos0 + t_idx * tt
    row_iota = ROW_STRIDE * lax.broadcasted_iota(jnp.int32, (SUBLANE, LANE), 0)
    for g, k in enumerate(POOL_WINDOWS):
        for b in range(bt):
            slab = b * pool_tiles + g
            for r0 in range(0, tt, CONV_ROWS):
                for rho in range(ROW_STRIDE):
                    first = POOL_PAD + r0 + rho
                    cur = ubuf[slab, pl.ds(first, SUBLANE, stride=ROW_STRIDE), :]
                    s = cur
                    for j in range(1, k):
                        s = s + ubuf[slab, pl.ds(first - j, SUBLANE, stride=ROW_STRIDE), :]
                    cnt = jnp.minimum(k, pos_base + (r0 + rho + 1) + row_iota).astype(_F32)
                    d_buf[g, pl.ds(b * tt + r0 + rho, SUBLANE, stride=ROW_STRIDE), :] = (
                        s / cnt - cur)
        lo, hi = g * POOL_GROUP, (g + 1) * POOL_GROUP
        za_buf[:, lo:hi] = (_dot(d_buf[g].astype(_BF16), w_grp[g])
                            * pool_scale[:, lo:hi]).astype(_BF16)

    n_ff = D_FF // FF_CHUNK
    n_iter = n_ff // FF_CHUNKS_PER_ITER
    ff_tiles = FF_CHUNK // LANE
    chunks_per_b = tt // CONV_ROWS
    conv_per_iter = bt * chunks_per_b // n_iter
    conv_base = CONV_PAD - CONV_HIST

    run_time_zero = jnp.minimum(step, 0)

    def conv_chunk(q):
        b = q // chunks_per_b
        r0 = (q % chunks_per_b) * CONV_ROWS + run_time_zero
        for lt in range(conv_tiles):
            lanes = slice(lt * LANE, (lt + 1) * LANE)
            slab = b * conv_tiles + lt
            accs = [None] * ROW_STRIDE
            for o in range(CONV_WIDTH + ROW_STRIDE - 1):
                d_o = vbuf[slab, pl.ds(r0 + (conv_base + o), SUBLANE, stride=ROW_STRIDE), :]
                for rho in range(ROW_STRIDE):
                    j = o - rho
                    if 0 <= j < CONV_WIDTH:
                        term = wdw_b[j, :, lanes] * d_o
                        accs[rho] = term if accs[rho] is None else accs[rho] + term
            for rho in range(ROW_STRIDE):
                vbuf[bt * conv_tiles + slab, pl.ds(r0 + rho, SUBLANE, stride=ROW_STRIDE), :] = (
                    accs[rho] + b_dw[:, lanes])

    f_buf[...] = jnp.zeros((m, D_MODEL), _F32)

    def middle(it, carry):
        h2 = h2_buf[...]
        acts = []
        for c in range(FF_CHUNKS_PER_ITER):
            first = (it * FF_CHUNKS_PER_ITER + c) * ff_tiles
            a = jnp.maximum(_dot(h2, _cols(w_ff1, first, ff_tiles)), 0.0)
            acts.append((a * a).astype(_BF16))
        f = f_buf[...]
        for c in range(FF_CHUNKS_PER_ITER):
            row0 = (it * FF_CHUNKS_PER_ITER + c) * FF_CHUNK
            f = f + _dot(acts[c], _cols(w_ff2, 0, d_tiles, slice(row0, row0 + FF_CHUNK)))
        f_buf[...] = f
        for c in range(conv_per_iter):
            conv_chunk(it * conv_per_iter + c)
        return carry

    for it in range(n_iter):
        middle(it, 0)

    _, _, gate_ffn = split3(mf_b_ref)
    y_ref[...] = y_ref[...] + _unit_rms(f_buf[...].reshape(bt, tt, D_MODEL)) * (
        gate_ffn * g_post_ffn[...])
    conv = jnp.concatenate(
        [jnp.concatenate([vbuf[(bt + b) * conv_tiles + lt, 0:tt, :]
                          for lt in range(conv_tiles)], axis=1) for b in range(bt)], axis=0)
    mu = jnp.mean(conv, axis=-1, keepdims=True)
    xc = conv - mu
    var = jnp.mean(xc * xc, axis=-1, keepdims=True)
    ln = xc * lax.rsqrt(var + EPS) * ln_g[...] + ln_b[...]
    z_b = (ln * _sigmoid(ln)).astype(_BF16)
    off = D_POOL + 2 * D_CONV
    h = h_buf[...]
    z_a = za_buf[...]
    half = D_MODEL // 2
    half_tiles = d_tiles // 2
    out = None
    for hf in range(2):
        c0 = hf * half
        g_a = _sigmoid(_dot(h, w_in[:, off + c0:off + c0 + half]))
        g_b = _sigmoid(_dot(h, w_in[:, off + D_MODEL + c0:off + D_MODEL + c0 + half]))
        y_a = _dot(z_a, _cols(w_pool_proj, hf * half_tiles, half_tiles))
        y_b = _dot(z_b, _cols(w_conv_proj, hf * half_tiles, half_tiles))
        merged = (g_a * y_a + g_b * y_b).astype(_BF16)
        part = _dot(merged, _cols(w_out, 0, d_tiles, slice(c0, c0 + half)))
        out = part if out is None else out + part
    out_carry[...] = out

    @pl.when(jnp.logical_and(t_idx == nt - 1, step < n_tiles))
    def _write_state():
        for b in range(bt):
            for lt in range(pool_tiles):
                npool_ref[b, :, lt * LANE:(lt + 1) * LANE] = (
                    ubuf[b * pool_tiles + lt, POOL_PAD + tt - POOL_HIST:POOL_PAD + tt, :])
            for lt in range(conv_tiles):
                nconv_ref[b, :, lt * LANE:(lt + 1) * LANE] = (
                    vbuf[b * conv_tiles + lt, CONV_PAD + tt - CONV_HIST:CONV_PAD + tt, :])

    ubuf[:, 0:POOL_PAD, :] = ubuf[:, tt:tt + POOL_PAD, :]
    vbuf[0:bt * conv_tiles, 0:CONV_PAD, :] = vbuf[0:bt * conv_tiles, tt:tt + CONV_PAD, :]


def _tile_rows(batch, seq):
    rows = TILE_ROWS if batch * seq >= MIN_TILES_FULL * TILE_ROWS else TILE_ROWS // 2
    tt = min(seq, rows)
    bt = max(1, min(batch, rows // tt))
    return bt, tt


def _layer(x, mod_mix, mod_ffn, state, pos0, params):
    batch, seq, _ = x.shape
    bt, tt = _tile_rows(batch, seq)
    n_iter = D_FF // FF_CHUNK // FF_CHUNKS_PER_ITER
    assert batch % bt == 0 and seq % tt == 0 and tt % CONV_ROWS == 0 and tt >= CONV_PAD
    assert (bt * tt // CONV_ROWS) % n_iter == 0
    has_state = state is not None
    m = bt * tt
    nt = seq // tt
    n_tiles = (batch // bt) * nt

    def tile_a(s):
        return jnp.minimum(s, n_tiles - 1)

    def tile_b(s):
        return jnp.maximum(s - 1, 0)

    def per_batch(rows, cols, tile):
        return pl.BlockSpec((bt, rows, cols), lambda s: (tile(s) // nt, 0, 0))

    def per_tile(tile):
        return pl.BlockSpec((bt, tt, D_MODEL), lambda s: (tile(s) // nt, tile(s) % nt, 0))

    def whole(a):
        nd = a.ndim
        return pl.BlockSpec(a.shape, lambda s: (0,) * nd, pipeline_mode=pl.Buffered(1))

    in_specs = [per_tile(tile_a), per_tile(tile_b), per_batch(1, 3 * D_MODEL, tile_a),
                per_batch(1, 3 * D_MODEL, tile_b), per_batch(1, 3 * D_MODEL, tile_b)]
    args = [x, x, mod_mix, mod_mix, mod_ffn]
    if has_state:
        in_specs += [per_batch(POOL_HIST, D_POOL, tile_a), per_batch(CONV_HIST, D_CONV, tile_a)]
        args += list(state)
    in_specs += [whole(p) for p in params]
    args += list(params)

    return pl.pallas_call(
        functools.partial(_layer_body, bt=bt, tt=tt, nt=nt, n_tiles=n_tiles, pos0=pos0,
                          has_state=has_state),
        grid=(n_tiles + 1,),
        in_specs=in_specs,
        out_specs=[per_tile(tile_b), per_batch(POOL_HIST, D_POOL, tile_a),
                   per_batch(CONV_HIST, D_CONV, tile_a)],
        out_shape=[jax.ShapeDtypeStruct(x.shape, _F32),
                   jax.ShapeDtypeStruct((batch, POOL_HIST, D_POOL), _F32),
                   jax.ShapeDtypeStruct((batch, CONV_HIST, D_CONV), _F32)],
        scratch_shapes=[pltpu.VMEM((bt * D_POOL // LANE, POOL_PAD + tt, LANE), _F32),
                        pltpu.VMEM((2 * bt * D_CONV // LANE, CONV_PAD + tt, LANE), _F32),
                        pltpu.VMEM((m, D_MODEL), _F32),
                        pltpu.VMEM((m, D_MODEL), _BF16),
                        pltpu.VMEM((m, D_MODEL), _BF16),
                        pltpu.VMEM((m, D_POOL), _BF16),
                        pltpu.VMEM((m, D_MODEL), _F32),
                        pltpu.VMEM((CONV_WIDTH, SUBLANE, D_CONV), _F32),
                        pltpu.VMEM((D_POOL // LANE, m, LANE), _F32)],
        compiler_params=pltpu.CompilerParams(
            dimension_semantics=("arbitrary",),
            vmem_limit_bytes=VMEM_LIMIT_BYTES),
        name="layer_state" if has_state else "layer",
    )(*args)


def kernel(x_prompt, x_sample, state_pool, state_conv, c_prompt, c_sample, w_ada_mix, b_ada_mix, g_pre_mix, g_post_mix, w_in, w_grp, pool_scale, w_pool_proj, w_dw, b_dw, ln_g, ln_b, w_conv_proj, w_out, w_ada_ffn, b_ada_ffn, g_pre_ffn, g_post_ffn, w_ff1, w_ff2):
    depth = w_in.shape[0]
    assert depth == 1
    l = 0
    bp = x_prompt.shape[0]

    c_all = jnp.concatenate([c_prompt, c_sample], axis=0)
    mod_mix, mod_ffn = _adaln(c_all, w_ada_mix[l], b_ada_mix[l][None, :],
                              w_ada_ffn[l], b_ada_ffn[l][None, :])
    mod_mix = mod_mix[:, None, :]
    mod_ffn = mod_ffn[:, None, :]

    row = lambda a: a[l][None, :]
    params = (row(g_pre_mix), row(g_post_mix), w_in[l].astype(_BF16), w_grp[l].astype(_BF16),
              row(pool_scale), _lane_major(w_pool_proj[l].astype(_BF16)), w_dw[l], row(b_dw),
              row(ln_g), row(ln_b), _lane_major(w_conv_proj[l].astype(_BF16)),
              _lane_major(w_out[l].astype(_BF16)), row(g_pre_ffn), row(g_post_ffn),
              _lane_major(w_ff1[l].astype(_BF16)), _lane_major(w_ff2[l].astype(_BF16)))

    y_p, pool_p, conv_p = _layer(x_prompt, mod_mix[:bp], mod_ffn[:bp], None, 0, params)
    y_s, pool_s, conv_s = _layer(x_sample, mod_mix[bp:], mod_ffn[bp:],
                                 (state_pool[l], state_conv[l]), PAST_LEN, params)
    return (y_p, y_s, pool_p[None], conv_p[None], pool_s[None], conv_s[None])
```

```python
import functools

import jax
import jax.numpy as jnp
from jax import lax
from jax.experimental import pallas as pl
from jax.experimental.pallas import tpu as pltpu

D_MODEL = 1024
POOL_WINDOWS = (2, 4, 8, 16)
POOL_GROUP = 128
D_POOL = POOL_GROUP * len(POOL_WINDOWS)
POOL_HIST = max(POOL_WINDOWS) - 1
D_CONV = 512
CONV_WIDTH = 31
CONV_HIST = CONV_WIDTH - 1
D_FF = 4 * D_MODEL
EPS = 1e-6
PAST_LEN = 1024

LANE = 128
SUBLANE = 8
POOL_PAD = 16
CONV_PAD = 32
ROW_STRIDE = 4
CONV_ROWS = ROW_STRIDE * SUBLANE
FF_CHUNK = 1024
FF_CHUNKS_PER_ITER = 2
ADA_TILE = 512
TILE_ROWS = 512
MIN_TILES_FULL = 16
VMEM_LIMIT_BYTES = 60 * 1024 * 1024

_F32 = jnp.float32
_BF16 = jnp.bfloat16


def _dot(a, b):
    return jnp.dot(a, b, preferred_element_type=_F32)


NEG_LOG2_E = -1.4426950408889634


def _sigmoid(x):
    return 1.0 / (1.0 + jnp.exp2(x * NEG_LOG2_E))


def _unit_rms(x):
    return x * lax.rsqrt(jnp.mean(x * x, axis=-1, keepdims=True) + EPS)


def _adaln_body(c_ref, wm_ref, bm_ref, wf_ref, bf_ref, om_ref, of_ref):
    c = c_ref[...]
    s = (c * _sigmoid(c)).astype(_BF16)
    om_ref[...] = _dot(s, wm_ref[...].astype(_BF16)) + bm_ref[...]
    of_ref[...] = _dot(s, wf_ref[...].astype(_BF16)) + bf_ref[...]


def _adaln(c, w_mix, b_mix, w_ffn, b_ffn):
    n = c.shape[0]
    n_out = w_mix.shape[1]
    w_spec = pl.BlockSpec((D_MODEL, ADA_TILE), lambda j: (0, j))
    b_spec = pl.BlockSpec((1, ADA_TILE), lambda j: (0, j))
    o_spec = pl.BlockSpec((n, ADA_TILE), lambda j: (0, j))
    return pl.pallas_call(
        _adaln_body,
        grid=(n_out // ADA_TILE,),
        in_specs=[pl.BlockSpec((n, D_MODEL), lambda j: (0, 0)), w_spec, b_spec, w_spec, b_spec],
        out_specs=[o_spec, o_spec],
        out_shape=[jax.ShapeDtypeStruct((n, n_out), _F32)] * 2,
        name="adaln",
    )(c, w_mix, b_mix, w_ffn, b_ffn)


def _lane_major(w):
    k, n = w.shape
    return w.reshape(k, n // LANE, LANE).transpose(1, 0, 2)


def _cols(w_ref, first_tile, n_tiles, rows=None):
    if rows is None:
        parts = [w_ref[first_tile + q] for q in range(n_tiles)]
    else:
        parts = [w_ref[first_tile + q, rows, :] for q in range(n_tiles)]
    return jnp.concatenate(parts, axis=1)


def _layer_body(*refs, bt, tt, nt, n_tiles, pos0, has_state):
    refs = list(refs)
    x_ref, xb_ref, mm_a_ref, mm_b_ref, mf_b_ref = refs[:5]
    refs = refs[5:]
    if has_state:
        sp_ref, sc_ref = refs[:2]
        refs = refs[2:]
    (g_pre_mix, g_post_mix, w_in, w_grp, pool_scale, w_pool_proj, w_dw, b_dw, ln_g, ln_b,
     w_conv_proj, w_out, g_pre_ffn, g_post_ffn, w_ff1, w_ff2,
     y_ref, npool_ref, nconv_ref,
     ubuf, vbuf, out_carry, h_buf, h2_buf, za_buf, f_buf, wdw_b, d_buf) = refs

    m = bt * tt
    step = pl.program_id(0)
    tile_a = jnp.minimum(step, n_tiles - 1)
    t_idx = tile_a % nt
    d_tiles = D_MODEL // LANE
    pool_tiles = D_POOL // LANE
    conv_tiles = D_CONV // LANE

    @pl.when(step == 0)
    def _init_carry():
        out_carry[...] = jnp.zeros((m, D_MODEL), _F32)
        for j in range(CONV_WIDTH):
            wdw_b[j] = jnp.broadcast_to(w_dw[j:j + 1, :], (SUBLANE, D_CONV))

    @pl.when(t_idx == 0)
    def _init_history():
        ubuf[:, 0:POOL_PAD, :] = jnp.zeros((bt * pool_tiles, POOL_PAD, LANE), _F32)
        vbuf[0:bt * conv_tiles, 0:CONV_PAD, :] = jnp.zeros((bt * conv_tiles, CONV_PAD, LANE), _F32)
        if has_state:
            for b in range(bt):
                for lt in range(pool_tiles):
                    ubuf[b * pool_tiles + lt, POOL_PAD - POOL_HIST:POOL_PAD, :] = (
                        sp_ref[b, :, lt * LANE:(lt + 1) * LANE])
                for lt in range(conv_tiles):
                    vbuf[b * conv_tiles + lt, CONV_PAD - CONV_HIST:CONV_PAD, :] = (
                        sc_ref[b, :, lt * LANE:(lt + 1) * LANE])

    def split3(mod_ref):
        mod = mod_ref[...]
        return (mod[:, :, 0:D_MODEL], mod[:, :, D_MODEL:2 * D_MODEL],
                mod[:, :, 2 * D_MODEL:3 * D_MODEL])

    x = x_ref[...]
    shift, scale, _ = split3(mm_a_ref)
    h = (_unit_rms(x) * (g_pre_mix[...] * (1.0 + scale)) + shift).astype(_BF16).reshape(m, D_MODEL)
    h_buf[...] = h

    u = _dot(h, w_in[:, 0:D_POOL])
    for b in range(bt):
        for lt in range(pool_tiles):
            ubuf[b * pool_tiles + lt, POOL_PAD:POOL_PAD + tt, :] = (
                u[b * tt:(b + 1) * tt, lt * LANE:(lt + 1) * LANE])
    v = _dot(h, w_in[:, D_POOL:D_POOL + D_CONV]) * _sigmoid(
        _dot(h, w_in[:, D_POOL + D_CONV:D_POOL + 2 * D_CONV]))
    for b in range(bt):
        for lt in range(conv_tiles):
            vbuf[b * conv_tiles + lt, CONV_PAD:CONV_PAD + tt, :] = (
                v[b * tt:(b + 1) * tt, lt * LANE:(lt + 1) * LANE])
    _, _, gate_mix = split3(mm_b_ref)
    out = out_carry[...].reshape(bt, tt, D_MODEL)
    x1 = xb_ref[...] + _unit_rms(out) * (gate_mix * g_post_mix[...])
    y_ref[...] = x1
    shift, scale, _ = split3(mf_b_ref)
    h2_buf[...] = (_unit_rms(x1) * (g_pre_ffn[...] * (1.0 + scale)) + shift).astype(
        _BF16).reshape(m, D_MODEL)

    pos_base = pos0 + t_idx * tt
    row_iota = ROW_STRIDE * lax.broadcasted_iota(jnp.int32, (SUBLANE, LANE), 0)
    for g, k in enumerate(POOL_WINDOWS):
        for b in range(bt):
            slab = b * pool_tiles + g
            for r0 in range(0, tt, CONV_ROWS):
                for rho in range(ROW_STRIDE):
                    first = POOL_PAD + r0 + rho
                    cur = ubuf[slab, pl.ds(first, SUBLANE, stride=ROW_STRIDE), :]
                    s = cur
                    for j in range(1, k):
                        s = s + ubuf[slab, pl.ds(first - j, SUBLANE, stride=ROW_STRIDE), :]
                    cnt = jnp.minimum(k, pos_base + (r0 + rho + 1) + row_iota).astype(_F32)
                    d_buf[g, pl.ds(b * tt + r0 + rho, SUBLANE, stride=ROW_STRIDE), :] = (
                        s / cnt - cur)
        lo, hi = g * POOL_GROUP, (g + 1) * POOL_GROUP
        za_buf[:, lo:hi] = (_dot(d_buf[g].astype(_BF16), w_grp[g])
                            * pool_scale[:, lo:hi]).astype(_BF16)

    n_ff = D_FF // FF_CHUNK
    n_iter = n_ff // FF_CHUNKS_PER_ITER
    ff_tiles = FF_CHUNK // LANE
    chunks_per_b = tt // CONV_ROWS
    conv_per_iter = bt * chunks_per_b // n_iter
    conv_base = CONV_PAD - CONV_HIST

    run_time_zero = jnp.minimum(step, 0)

    def conv_chunk(q):
        b = q // chunks_per_b
        r0 = (q % chunks_per_b) * CONV_ROWS + run_time_zero
        for lt in range(conv_tiles):
            lanes = slice(lt * LANE, (lt + 1) * LANE)
            slab = b * conv_tiles + lt
            accs = [None] * ROW_STRIDE
            for o in range(CONV_WIDTH + ROW_STRIDE - 1):
                d_o = vbuf[slab, pl.ds(r0 + (conv_base + o), SUBLANE, stride=ROW_STRIDE), :]
                for rho in range(ROW_STRIDE):
                    j = o - rho
                    if 0 <= j < CONV_WIDTH:
                        term = wdw_b[j, :, lanes] * d_o
                        accs[rho] = term if accs[rho] is None else accs[rho] + term
            for rho in range(ROW_STRIDE):
                vbuf[bt * conv_tiles + slab, pl.ds(r0 + rho, SUBLANE, stride=ROW_STRIDE), :] = (
                    accs[rho] + b_dw[:, lanes])

    f_buf[...] = jnp.zeros((m, D_MODEL), _F32)

    def middle(it, carry):
        h2 = h2_buf[...]
        acts = []
        for c in range(FF_CHUNKS_PER_ITER):
            first = (it * FF_CHUNKS_PER_ITER + c) * ff_tiles
            a = jnp.maximum(_dot(h2, _cols(w_ff1, first, ff_tiles)), 0.0)
            acts.append((a * a).astype(_BF16))
            if c == 0:
                vbuf[2 * bt * conv_tiles, pl.ds(run_time_zero, SUBLANE), :] = a[0:SUBLANE, 0:LANE]
        f = f_buf[...]
        for c in range(FF_CHUNKS_PER_ITER):
            row0 = (it * FF_CHUNKS_PER_ITER + c) * FF_CHUNK
            f = f + _dot(acts[c], _cols(w_ff2, 0, d_tiles, slice(row0, row0 + FF_CHUNK)))
        f_buf[...] = f
        for c in range(conv_per_iter):
            conv_chunk(it * conv_per_iter + c)
        return carry

    for it in range(n_iter):
        middle(it, 0)

    _, _, gate_ffn = split3(mf_b_ref)
    y_ref[...] = y_ref[...] + _unit_rms(f_buf[...].reshape(bt, tt, D_MODEL)) * (
        gate_ffn * g_post_ffn[...])
    conv = jnp.concatenate(
        [jnp.concatenate([vbuf[(bt + b) * conv_tiles + lt, 0:tt, :]
                          for lt in range(conv_tiles)], axis=1) for b in range(bt)], axis=0)
    mu = jnp.mean(conv, axis=-1, keepdims=True)
    xc = conv - mu
    var = jnp.mean(xc * xc, axis=-1, keepdims=True)
    ln = xc * lax.rsqrt(var + EPS) * ln_g[...] + ln_b[...]
    z_b = (ln * _sigmoid(ln)).astype(_BF16)
    off = D_POOL + 2 * D_CONV
    h = h_buf[...]
    z_a = za_buf[...]
    half = D_MODEL // 2
    half_tiles = d_tiles // 2
    out = None
    for hf in range(2):
        c0 = hf * half
        g_a = _sigmoid(_dot(h, w_in[:, off + c0:off + c0 + half]))
        g_b = _sigmoid(_dot(h, w_in[:, off + D_MODEL + c0:off + D_MODEL + c0 + half]))
        y_a = _dot(z_a, _cols(w_pool_proj, hf * half_tiles, half_tiles))
        y_b = _dot(z_b, _cols(w_conv_proj, hf * half_tiles, half_tiles))
        merged = (g_a * y_a + g_b * y_b).astype(_BF16)
        part = _dot(merged, _cols(w_out, 0, d_tiles, slice(c0, c0 + half)))
        out = part if out is None else out + part
    out_carry[...] = out

    @pl.when(jnp.logical_and(t_idx == nt - 1, step < n_tiles))
    def _write_state():
        for b in range(bt):
            for lt in range(pool_tiles):
                npool_ref[b, :, lt * LANE:(lt + 1) * LANE] = (
                    ubuf[b * pool_tiles + lt, POOL_PAD + tt - POOL_HIST:POOL_PAD + tt, :])
            for lt in range(conv_tiles):
                nconv_ref[b, :, lt * LANE:(lt + 1) * LANE] = (
                    vbuf[b * conv_tiles + lt, CONV_PAD + tt - CONV_HIST:CONV_PAD + tt, :])

    ubuf[:, 0:POOL_PAD, :] = ubuf[:, tt:tt + POOL_PAD, :]
    vbuf[0:bt * conv_tiles, 0:CONV_PAD, :] = vbuf[0:bt * conv_tiles, tt:tt + CONV_PAD, :]


def _tile_rows(batch, seq):
    rows = TILE_ROWS if batch * seq >= MIN_TILES_FULL * TILE_ROWS else TILE_ROWS // 2
    tt = min(seq, rows)
    bt = max(1, min(batch, rows // tt))
    return bt, tt


def _layer(x, mod_mix, mod_ffn, state, pos0, params):
    batch, seq, _ = x.shape
    bt, tt = _tile_rows(batch, seq)
    n_iter = D_FF // FF_CHUNK // FF_CHUNKS_PER_ITER
    assert batch % bt == 0 and seq % tt == 0 and tt % CONV_ROWS == 0 and tt >= CONV_PAD
    assert (bt * tt // CONV_ROWS) % n_iter == 0
    has_state = state is not None
    m = bt * tt
    nt = seq // tt
    n_tiles = (batch // bt) * nt

    def tile_a(s):
        return jnp.minimum(s, n_tiles - 1)

    def tile_b(s):
        return jnp.maximum(s - 1, 0)

    def per_batch(rows, cols, tile):
        return pl.BlockSpec((bt, rows, cols), lambda s: (tile(s) // nt, 0, 0))

    def per_tile(tile):
        return pl.BlockSpec((bt, tt, D_MODEL), lambda s: (tile(s) // nt, tile(s) % nt, 0))

    def whole(a):
        nd = a.ndim
        return pl.BlockSpec(a.shape, lambda s: (0,) * nd, pipeline_mode=pl.Buffered(1))

    in_specs = [per_tile(tile_a), per_tile(tile_b), per_batch(1, 3 * D_MODEL, tile_a),
                per_batch(1, 3 * D_MODEL, tile_b), per_batch(1, 3 * D_MODEL, tile_b)]
    args = [x, x, mod_mix, mod_mix, mod_ffn]
    if has_state:
        in_specs += [per_batch(POOL_HIST, D_POOL, tile_a), per_batch(CONV_HIST, D_CONV, tile_a)]
        args += list(state)
    in_specs += [whole(p) for p in params]
    args += list(params)

    return pl.pallas_call(
        functools.partial(_layer_body, bt=bt, tt=tt, nt=nt, n_tiles=n_tiles, pos0=pos0,
                          has_state=has_state),
        grid=(n_tiles + 1,),
        in_specs=in_specs,
        out_specs=[per_tile(tile_b), per_batch(POOL_HIST, D_POOL, tile_a),
                   per_batch(CONV_HIST, D_CONV, tile_a)],
        out_shape=[jax.ShapeDtypeStruct(x.shape, _F32),
                   jax.ShapeDtypeStruct((batch, POOL_HIST, D_POOL), _F32),
                   jax.ShapeDtypeStruct((batch, CONV_HIST, D_CONV), _F32)],
        scratch_shapes=[pltpu.VMEM((bt * D_POOL // LANE, POOL_PAD + tt, LANE), _F32),
                        pltpu.VMEM((2 * bt * D_CONV // LANE + 1, CONV_PAD + tt, LANE), _F32),
                        pltpu.VMEM((m, D_MODEL), _F32),
                        pltpu.VMEM((m, D_MODEL), _BF16),
                        pltpu.VMEM((m, D_MODEL), _BF16),
                        pltpu.VMEM((m, D_POOL), _BF16),
                        pltpu.VMEM((m, D_MODEL), _F32),
                        pltpu.VMEM((CONV_WIDTH, SUBLANE, D_CONV), _F32),
                        pltpu.VMEM((D_POOL // LANE, m, LANE), _F32)],
        compiler_params=pltpu.CompilerParams(
            dimension_semantics=("arbitrary",),
            vmem_limit_bytes=VMEM_LIMIT_BYTES),
        name="layer_state" if has_state else "layer",
    )(*args)


def kernel(x_prompt, x_sample, state_pool, state_conv, c_prompt, c_sample, w_ada_mix, b_ada_mix, g_pre_mix, g_post_mix, w_in, w_grp, pool_scale, w_pool_proj, w_dw, b_dw, ln_g, ln_b, w_conv_proj, w_out, w_ada_ffn, b_ada_ffn, g_pre_ffn, g_post_ffn, w_ff1, w_ff2):
    depth = w_in.shape[0]
    assert depth == 1
    l = 0
    bp = x_prompt.shape[0]

    c_all = jnp.concatenate([c_prompt, c_sample], axis=0)
    mod_mix, mod_ffn = _adaln(c_all, w_ada_mix[l], b_ada_mix[l][None, :],
                              w_ada_ffn[l], b_ada_ffn[l][None, :])
    mod_mix = mod_mix[:, None, :]
    mod_ffn = mod_ffn[:, None, :]

    row = lambda a: a[l][None, :]
    params = (row(g_pre_mix), row(g_post_mix), w_in[l].astype(_BF16), w_grp[l].astype(_BF16),
              row(pool_scale), _lane_major(w_pool_proj[l].astype(_BF16)), w_dw[l], row(b_dw),
              row(ln_g), row(ln_b), _lane_major(w_conv_proj[l].astype(_BF16)),
              _lane_major(w_out[l].astype(_BF16)), row(g_pre_ffn), row(g_post_ffn),
              _lane_major(w_ff1[l].astype(_BF16)), _lane_major(w_ff2[l].astype(_BF16)))

    y_p, pool_p, conv_p = _layer(x_prompt, mod_mix[:bp], mod_ffn[:bp], None, 0, params)
    y_s, pool_s, conv_s = _layer(x_sample, mod_mix[bp:], mod_ffn[bp:],
                                 (state_pool[l], state_conv[l]), PAST_LEN, params)
    return (y_p, y_s, pool_p[None], conv_p[None], pool_s[None], conv_s[None])
```
